```python
import jax
import jax.numpy as jnp
from jax import lax
import numpy as np

D_MODEL = 1024
BATCH = 16
SEQ = 2048
DEPTH = 2

GRID_W = 64
ROPE_THETA = 10000.0
NORM_EPS = 1e-6
Q_BLOCK = 128
N_BRANCH = 4

SSD_HEADS = 16
SSD_HEAD_DIM = 64
SSD_INNER = SSD_HEADS * SSD_HEAD_DIM
SSD_GROUPS = 2
SSD_STATE = 128
SSD_CONV = 5
SSD_CHUNK = 128
SSD_CONV_DIM = SSD_INNER + 2 * SSD_GROUPS * SSD_STATE

MLA_HEADS = 8
MLA_Q_RANK = 384
MLA_KV_RANK = 256
MLA_NOPE = 64
MLA_ROPE = 32
MLA_V = 64
MLA_WIDTH = MLA_HEADS * MLA_V

GLA_HEADS = 4
GLA_DK = 64
GLA_DV = 128
GLA_GATE_RANK = 16
GLA_TAU = 16.0
GLA_CHUNK = 64
GLA_WIDTH = GLA_HEADS * GLA_DV

GQA_HEADS = 8
GQA_KV_HEADS = 2
GQA_HEAD_DIM = 64
GQA_WIDTH = GQA_HEADS * GQA_HEAD_DIM

IN_WIDTHS = (
    N_BRANCH * D_MODEL,
    SSD_INNER,
    SSD_CONV_DIM,
    2 * SSD_HEADS,
    MLA_WIDTH,
    MLA_Q_RANK,
    MLA_KV_RANK,
    MLA_ROPE,
    GLA_WIDTH,
    GLA_HEADS * GLA_DK,
    GLA_HEADS * GLA_DK,
    GLA_WIDTH,
    2 * GLA_GATE_RANK,
    GQA_WIDTH,
    GQA_WIDTH,
    GQA_KV_HEADS * GQA_HEAD_DIM,
    GQA_KV_HEADS * GQA_HEAD_DIM,
)
N_IN = sum(IN_WIDTHS)

kernel_name = 'hybrid_gated_branch_encoder'


def rmsnorm(x, g):
    xf = x.astype(jnp.float32)
    y = xf * lax.rsqrt(jnp.mean(xf * xf, axis=-1, keepdims=True) + NORM_EPS)
    return (y * g.astype(jnp.float32)).astype(x.dtype)


def rev(t):
    return jnp.flip(t, axis=1)


def axial_rope_tables(rows, d_rot):
    row = jnp.repeat(jnp.arange(rows), GRID_W).astype(jnp.float32)
    col = jnp.tile(jnp.arange(GRID_W), rows).astype(jnp.float32)
    m = d_rot // 2
    inv = ROPE_THETA ** (-jnp.arange(0, m, 2, dtype=jnp.float32) / m)
    ang_r = row[:, None] * inv
    ang_c = col[:, None] * inv
    ang = jnp.concatenate([ang_r, ang_r, ang_c, ang_c], axis=-1)
    return jnp.cos(ang), jnp.sin(ang)


def apply_axial_rope(x, cos, sin):
    d = x.shape[-1]
    m = d // 2
    hm = m // 2
    xf = x.astype(jnp.float32)
    x1 = xf[..., :m]
    x2 = xf[..., m:]
    rot = jnp.concatenate([-x1[..., hm:], x1[..., :hm], -x2[..., hm:], x2[..., :hm]], axis=-1)
    return (xf * cos[None, :, None, :] + rot * sin[None, :, None, :]).astype(x.dtype)


def block_attention(q, k, v, scale):
    b, L, hq, d = q.shape
    hk = k.shape[2]
    r = hq // hk
    dv = v.shape[-1]
    nb = L // Q_BLOCK
    qb = q.reshape(b, nb, Q_BLOCK, hk, r, d).transpose(1, 0, 2, 3, 4, 5)

    def one_block(qblk):
        s = jnp.einsum('bqgrd,bkgd->bgrqk', qblk, k).astype(jnp.float32) * scale
        p = jax.nn.softmax(s, axis=-1).astype(v.dtype)
        return jnp.einsum('bgrqk,bkgv->bqgrv', p, v)

    o = lax.map(one_block, qb)
    return o.transpose(1, 0, 2, 3, 4, 5).reshape(b, L, hq * dv)


def centred_depthwise_conv(u, w, bias):
    pad = (SSD_CONV - 1) // 2
    y = lax.conv_general_dilated(u, w[:, None, :].astype(u.dtype), window_strides=(1,),
                                 padding=[(pad, pad)], dimension_numbers=('NWC', 'WIO', 'NWC'),
                                 feature_group_count=u.shape[-1])
    return y + bias.astype(u.dtype)


def ssd_chunked(x, dt, a_neg, bm, cm):
    b, L, H, P = x.shape
    G, N = bm.shape[2], bm.shape[3]
    E = H // G
    Q = SSD_CHUNK
    nc = L // Q
    xd = (x.astype(jnp.float32) * dt[..., None]).astype(x.dtype).reshape(b, nc, Q, G, E, P)
    a_cum = jnp.cumsum((dt * a_neg).reshape(b, nc, Q, G, E), axis=2)
    bc = bm.reshape(b, nc, Q, G, N)
    cc = cm.reshape(b, nc, Q, G, N)
    tri = jnp.tril(jnp.ones((Q, Q), dtype=bool))
    seg = a_cum[:, :, :, None] - a_cum[:, :, None, :]
    decay = jnp.exp(jnp.where(tri[None, None, :, :, None, None], seg, -jnp.inf)).astype(x.dtype)
    cb = jnp.einsum('bclgn,bcsgn->bclsg', cc, bc)
    y_diag = jnp.einsum('bclsge,bcsgep->bclgep', cb[..., None] * decay, xd)
    decay_end = jnp.exp(a_cum[:, :, -1:] - a_cum).astype(x.dtype)
    states = jnp.einsum('bcsgn,bcsgep->bcgepn', bc, xd * decay_end[..., None])
    chunk_decay = jnp.exp(a_cum[:, :, -1]).astype(x.dtype)

    def step(s, inp):
        dec, st = inp
        return dec[..., None, None] * s + st, s

    s0 = jnp.zeros_like(states[:, 0])
    _, s_prev = lax.scan(step, s0, (chunk_decay.transpose(1, 0, 2, 3),
                                    states.transpose(1, 0, 2, 3, 4, 5)))
    s_prev = s_prev.transpose(1, 0, 2, 3, 4, 5)
    y_off = jnp.einsum('bclgn,bcgepn->bclgep', cc, s_prev) * jnp.exp(a_cum).astype(x.dtype)[..., None]
    return (y_diag + y_off).reshape(b, L, H, P)


def gla_chunked(q, k, v, g_log):
    b, L, H, K = q.shape
    V = v.shape[-1]
    Q = GLA_CHUNK
    nc = L // Q
    g_cum = jnp.cumsum(g_log.reshape(b, nc, Q, H, K), axis=2)
    qc = q.reshape(b, nc, Q, H, K).astype(jnp.float32)
    kc = k.reshape(b, nc, Q, H, K).astype(jnp.float32)
    vc = v.reshape(b, nc, Q, H, V)
    qg = (qc * jnp.exp(g_cum)).astype(q.dtype)
    kg = (kc * jnp.exp(-g_cum)).astype(q.dtype)
    k_end = (kc * jnp.exp(g_cum[:, :, -1:] - g_cum)).astype(q.dtype)
    tri = jnp.tril(jnp.ones((Q, Q), dtype=bool))
    att = jnp.einsum('bclhk,bcshk->bchls', qg, kg)
    att = jnp.where(tri, att, jnp.zeros((), att.dtype))
    o_intra = jnp.einsum('bchls,bcshv->bclhv', att, vc)
    u = jnp.einsum('bcshk,bcshv->bchkv', k_end, vc)
    chunk_decay = jnp.exp(g_cum[:, :, -1]).astype(q.dtype)

    def step(s, inp):
        dec, st = inp
        return dec[..., None] * s + st, s

    s0 = jnp.zeros_like(u[:, 0])
    _, s_prev = lax.scan(step, s0, (chunk_decay.transpose(1, 0, 2, 3),
                                    u.transpose(1, 0, 2, 3, 4)))
    s_prev = s_prev.transpose(1, 0, 2, 3, 4)
    o_inter = jnp.einsum('bclhk,bchkv->bclhv', qg, s_prev)
    return (o_intra + o_inter).reshape(b, L, H, V)


def ssd_branch(z, xbc, dt_raw, conv_w, conv_b, a_log, dt_bias, d_skip, norm_g, w_br):
    b, L, _ = xbc.shape
    xbc = jax.nn.silu(centred_depthwise_conv(xbc, conv_w, conv_b))
    xs, bm, cm = jnp.split(xbc, [SSD_INNER, SSD_INNER + SSD_GROUPS * SSD_STATE], axis=-1)
    xs = xs.reshape(b, L, SSD_HEADS, SSD_HEAD_DIM)
    bm = bm.reshape(b, L, SSD_GROUPS, SSD_STATE)
    cm = cm.reshape(b, L, SSD_GROUPS, SSD_STATE)
    dt = jax.nn.softplus(dt_raw.astype(jnp.float32).reshape(b, L, 2, SSD_HEADS)
                         + dt_bias.astype(jnp.float32))
    a_neg = -jnp.exp(a_log.astype(jnp.float32))
    y_f = ssd_chunked(xs, dt[:, :, 0], a_neg[0], bm, cm)
    y_b = rev(ssd_chunked(rev(xs), rev(dt[:, :, 1]), a_neg[1], rev(bm), rev(cm)))
    y = (y_f + y_b + xs * d_skip[:, None].astype(xs.dtype)).reshape(b, L, SSD_INNER)
    return rmsnorm(y * jax.nn.silu(z), norm_g) @ w_br


def mla_branch(z, q_lat, kv_lat, k_rope, q_lat_norm_g, kv_lat_norm_g, w_q_b, w_kv_b, w_br, cos, sin):
    b, L, _ = q_lat.shape
    q = (rmsnorm(q_lat, q_lat_norm_g) @ w_q_b).reshape(b, L, MLA_HEADS, MLA_NOPE + MLA_ROPE)
    q = jnp.concatenate([q[..., :MLA_NOPE], apply_axial_rope(q[..., MLA_NOPE:], cos, sin)], axis=-1)
    kv = (rmsnorm(kv_lat, kv_lat_norm_g) @ w_kv_b).reshape(b, L, MLA_HEADS, MLA_NOPE + MLA_V)
    k_nope = kv[..., :MLA_NOPE]
    v = kv[..., MLA_NOPE:]
    k_r = apply_axial_rope(k_rope.reshape(b, L, 1, MLA_ROPE), cos, sin)
    k = jnp.concatenate([k_nope, jnp.broadcast_to(k_r, (b, L, MLA_HEADS, MLA_ROPE))], axis=-1)
    o = block_attention(q, k, v, (MLA_NOPE + MLA_ROPE) ** -0.5)
    return (o * jax.nn.silu(z)) @ w_br


def gla_branch(z, q_c, k_c, v_c, g_lr, w_gate_up, b_gate, norm_g, w_br):
    b, L, _ = q_c.shape
    q = q_c.reshape(b, L, GLA_HEADS, GLA_DK) * (GLA_DK ** -0.5)
    k = k_c.reshape(b, L, GLA_HEADS, GLA_DK)
    v = v_c.reshape(b, L, GLA_HEADS, GLA_DV)
    g_pre = jnp.einsum('blnr,nrk->blnk', g_lr.astype(jnp.float32).reshape(b, L, 2, GLA_GATE_RANK),
                       w_gate_up.astype(jnp.float32)) + b_gate.astype(jnp.float32)
    g_log = jax.nn.log_sigmoid(g_pre) / GLA_TAU
    g_f = g_log[:, :, 0].reshape(b, L, GLA_HEADS, GLA_DK)
    g_b = g_log[:, :, 1].reshape(b, L, GLA_HEADS, GLA_DK)
    o = gla_chunked(q, k, v, g_f) + rev(gla_chunked(rev(q), rev(k), rev(v), rev(g_b)))
    o = rmsnorm(o, norm_g.reshape(GLA_HEADS, GLA_DV)).reshape(b, L, GLA_WIDTH)
    return (o * jax.nn.silu(z)) @ w_br


def gqa_branch(z, q_d, k_d, v_d, q_norm_g, k_norm_g, w_br, cos, sin):
    b, L, _ = q_d.shape
    q = rmsnorm(q_d.reshape(b, L, GQA_HEADS, GQA_HEAD_DIM), q_norm_g)
    k = rmsnorm(k_d.reshape(b, L, GQA_KV_HEADS, GQA_HEAD_DIM), k_norm_g)
    q = apply_axial_rope(q, cos, sin)
    k = apply_axial_rope(k, cos, sin)
    v = v_d.reshape(b, L, GQA_KV_HEADS, GQA_HEAD_DIM)
    o = block_attention(q, k, v, GQA_HEAD_DIM ** -0.5)
    return (o * jax.nn.silu(z)) @ w_br


def setup_inputs(seed: int = 0) -> dict:
    key = jax.random.key(seed)
    ks = jax.random.split(key, 32)
    f32 = jnp.float32

    def nrm(k, shape, scale):
        return jax.random.normal(k, shape, f32) * scale

    def gain(k, shape):
        return 1.0 + 0.02 * jax.random.normal(k, shape, f32)

    dt0 = jnp.exp(jax.random.uniform(ks[5], (DEPTH, 2, SSD_HEADS), f32, np.log(1e-3), np.log(1e-1)))
    return {
        'x': jax.random.normal(ks[0], (BATCH, SEQ, D_MODEL), f32),
        'norm_g': gain(ks[1], (DEPTH, D_MODEL)),
        'w_in': nrm(ks[2], (DEPTH, D_MODEL, N_IN), D_MODEL ** -0.5),
        'conv_w': nrm(ks[3], (DEPTH, SSD_CONV, SSD_CONV_DIM), SSD_CONV ** -0.5),
        'conv_b': nrm(ks[4], (DEPTH, SSD_CONV_DIM), 0.01),
        'a_log': jnp.log(jax.random.uniform(ks[6], (DEPTH, 2, SSD_HEADS), f32, 1.0, 16.0)),
        'dt_bias': dt0 + jnp.log(-jnp.expm1(-dt0)),
        'd_skip': gain(ks[7], (DEPTH, SSD_HEADS)),
        'ssd_norm_g': gain(ks[8], (DEPTH, SSD_INNER)),
        'q_lat_norm_g': gain(ks[9], (DEPTH, MLA_Q_RANK)),
        'kv_lat_norm_g': gain(ks[10], (DEPTH, MLA_KV_RANK)),
        'w_q_b': nrm(ks[11], (DEPTH, MLA_Q_RANK, MLA_HEADS * (MLA_NOPE + MLA_ROPE)), MLA_Q_RANK ** -0.5),
        'w_kv_b': nrm(ks[12], (DEPTH, MLA_KV_RANK, MLA_HEADS * (MLA_NOPE + MLA_V)), MLA_KV_RANK ** -0.5),
        'w_gate_up': nrm(ks[13], (DEPTH, 2, GLA_GATE_RANK, GLA_HEADS * GLA_DK), GLA_GATE_RANK ** -0.5),
        'b_gate': nrm(ks[14], (DEPTH, 2, GLA_HEADS * GLA_DK), 0.01),
        'gla_norm_g': gain(ks[15], (DEPTH, GLA_WIDTH)),
        'q_norm_g': gain(ks[16], (DEPTH, GQA_HEAD_DIM)),
        'k_norm_g': gain(ks[17], (DEPTH, GQA_HEAD_DIM)),
        'w_br_a': nrm(ks[18], (DEPTH, SSD_INNER, D_MODEL), SSD_INNER ** -0.5),
        'w_br_b': nrm(ks[19], (DEPTH, MLA_WIDTH, D_MODEL), MLA_WIDTH ** -0.5),
        'w_br_c': nrm(ks[20], (DEPTH, GLA_WIDTH, D_MODEL), GLA_WIDTH ** -0.5),
        'w_br_d': nrm(ks[21], (DEPTH, GQA_WIDTH, D_MODEL), GQA_WIDTH ** -0.5),
        'w_out': nrm(ks[22], (DEPTH, D_MODEL, D_MODEL), 0.5 * D_MODEL ** -0.5),
        'final_g': gain(ks[23], (D_MODEL,)),
    }


def reference(x, norm_g, w_in, conv_w, conv_b, a_log, dt_bias, d_skip, ssd_norm_g,
              q_lat_norm_g, kv_lat_norm_g, w_q_b, w_kv_b, w_gate_up, b_gate, gla_norm_g,
              q_norm_g, k_norm_g, w_br_a, w_br_b, w_br_c, w_br_d, w_out, final_g):
    b, L, _ = x.shape
    rows = L // GRID_W
    cos_m, sin_m = axial_rope_tables(rows, MLA_ROPE)
    cos_g, sin_g = axial_rope_tables(rows, GQA_HEAD_DIM)
    split_idx = [int(v) for v in np.cumsum(IN_WIDTHS)[:-1]]
    for i in range(DEPTH):
        h = rmsnorm(x, norm_g[i])
        u = h @ w_in[i]
        (g_merge, z_a, xbc, dt_raw, z_b, q_lat, kv_lat, k_rope, z_c, q_c, k_c, v_c, g_lr,
         z_d, q_d, k_d, v_d) = jnp.split(u, split_idx, axis=-1)
        y_a = ssd_branch(z_a, xbc, dt_raw, conv_w[i], conv_b[i], a_log[i], dt_bias[i], d_skip[i],
                         ssd_norm_g[i], w_br_a[i])
        y_b = mla_branch(z_b, q_lat, kv_lat, k_rope, q_lat_norm_g[i], kv_lat_norm_g[i], w_q_b[i],
                         w_kv_b[i], w_br_b[i], cos_m, sin_m)
        y_c = gla_branch(z_c, q_c, k_c, v_c, g_lr, w_gate_up[i], b_gate[i], gla_norm_g[i], w_br_c[i])
        y_d = gqa_branch(z_d, q_d, k_d, v_d, q_norm_g[i], k_norm_g[i], w_br_d[i], cos_g, sin_g)
        gates = jax.nn.sigmoid(g_merge.astype(jnp.float32)).astype(x.dtype).reshape(b, L, N_BRANCH, D_MODEL)
        mixed = gates[:, :, 0] * y_a + gates[:, :, 1] * y_b + gates[:, :, 2] * y_c + gates[:, :, 3] * y_d
        x = x + mixed @ w_out[i]
    return rmsnorm(x, final_g)
```

```python
import functools

import numpy as np
import jax
import jax.numpy as jnp
from jax import lax
from jax.experimental import pallas as pl
from jax.experimental.pallas import tpu as pltpu

F32 = jnp.float32
BF16 = jnp.bfloat16

D_MODEL = 1024
GRID_W = 64
ROPE_THETA = 10000.0
NORM_EPS = 1e-6
N_BRANCH = 4

SSD_HEADS = 16
SSD_HEAD_DIM = 64
SSD_INNER = SSD_HEADS * SSD_HEAD_DIM
SSD_GROUPS = 2
SSD_STATE = 128
SSD_CONV = 5
SSD_CHUNK = 128
SSD_BC = SSD_GROUPS * SSD_STATE

MLA_HEADS = 8
MLA_Q_RANK = 384
MLA_KV_RANK = 256
MLA_NOPE = 64
MLA_ROPE = 32
MLA_V = 64
MLA_WIDTH = MLA_HEADS * MLA_V

GLA_HEADS = 4
GLA_DK = 64
GLA_DV = 128
GLA_GATE_RANK = 16
GLA_TAU = 16.0
GLA_CHUNK = 64
GLA_WIDTH = GLA_HEADS * GLA_DV
GLA_QK = GLA_HEADS * GLA_DK

GQA_HEADS = 8
GQA_KV_HEADS = 2
GQA_HEAD_DIM = 64
GQA_WIDTH = GQA_HEADS * GQA_HEAD_DIM
GQA_REP = GQA_HEADS // GQA_KV_HEADS
GQA_GROUP_W = GQA_REP * GQA_HEAD_DIM

IN_WIDTHS = (
    N_BRANCH * D_MODEL, SSD_INNER, SSD_INNER + 2 * SSD_BC, 2 * SSD_HEADS, MLA_WIDTH, MLA_Q_RANK,
    MLA_KV_RANK, MLA_ROPE, GLA_WIDTH, GLA_QK, GLA_QK, GLA_WIDTH, 2 * GLA_GATE_RANK, GQA_WIDTH,
    GQA_WIDTH, GQA_KV_HEADS * GQA_HEAD_DIM, GQA_KV_HEADS * GQA_HEAD_DIM,
)

LANE = 128

SMALL_DT = 0
SMALL_GLR = 32
SMALL_KROPE = 64
COL_GMERGE = 0
COL_ZA = 4096
COL_XS = 5120
COL_ZB = 6144
COL_ZC = 6656
COL_ZD = 7168
COL_QD = 7680
COL_VC = 8192
COL_QLAT = 8704
COL_SMALL = COL_QLAT + MLA_Q_RANK
COL_BM = 9216
COL_CM = 9472
COL_KVLAT = 9728
COL_QC = 9984
COL_KC = 10240
COL_KD = 10496
N_PROJ = 10752

VMEM_LIMIT = 56 * 1024 * 1024

NN = ((1,), (0,))
NT = ((1,), (1,))
TN = ((0,), (0,))


def _dot(a, b, dims=NN):
    return lax.dot_general(a, b, (dims, ((), ())), preferred_element_type=F32)


def _split(a):
    hi = a.astype(BF16)
    lo = (a - hi.astype(F32)).astype(BF16)
    return hi, lo


def _dot2(a, m, dims=NN):
    hi, lo = _split(a)
    return _dot(hi, m, dims) + _dot(lo, m, dims)


def _dot2r(m, a, dims=NN):
    hi, lo = _split(a)
    return _dot(m, hi, dims) + _dot(m, lo, dims)


def _softplus(x):
    return jnp.maximum(x, 0.0) + jnp.log1p(jnp.exp(-jnp.abs(x)))


def _silu(x):
    return x * (1.0 / (1.0 + jnp.exp(-x)))


def _sigmoid(x):
    return 1.0 / (1.0 + jnp.exp(-x))


def _params(*sem):
    return pltpu.CompilerParams(dimension_semantics=sem, vmem_limit_bytes=VMEM_LIMIT)


def _inproj_kernel(x_ref, g_ref, w_ref, o_ref, h_ref):
    @pl.when(pl.program_id(1) == 0)
    def _():
        x = x_ref[...]
        ms = jnp.mean(x * x, axis=-1, keepdims=True)
        h_ref[...] = (x * lax.rsqrt(ms + NORM_EPS) * g_ref[...]).astype(BF16)

    o_ref[...] = _dot(h_ref[...], w_ref[...]).astype(BF16)


def _inproj(x2, g, w):
    t = x2.shape[0]
    tm = min(1024, t)
    tn = 1536
    return pl.pallas_call(
        _inproj_kernel,
        grid=(t // tm, N_PROJ // tn),
        in_specs=[
            pl.BlockSpec((tm, D_MODEL), lambda i, j: (i, 0)),
            pl.BlockSpec((1, D_MODEL), lambda i, j: (0, 0)),
            pl.BlockSpec((D_MODEL, tn), lambda i, j: (0, j)),
        ],
        out_specs=pl.BlockSpec((tm, tn), lambda i, j: (i, j)),
        out_shape=jax.ShapeDtypeStruct((t, N_PROJ), BF16),
        scratch_shapes=[pltpu.VMEM((tm, D_MODEL), BF16)],
        compiler_params=_params("parallel", "arbitrary"),
        name="inproj",
    )(x2, g, w)


CONV_W = 256
CONV_HALO = 8


def _ssd_kernel(xs_ref, bm_ref, cm_ref, sm_ref, cwx_ref, cwb_ref, cwc_ref, cbx_ref, cbb_ref, cbc_ref,
                dtb_ref, aneg_ref, dskip_ref, tri_ref, exp_ref, o_ref,
                pad_s, xs_s, bm_s, cm_s, dt_s, yacc_s, st_s):
    L = xs_ref.shape[0]
    Q = SSD_CHUNK
    nc = L // Q
    ext = Q + 2 * CONV_HALO

    pad_s[0:CONV_HALO, :] = jnp.zeros((CONV_HALO, CONV_W), F32)
    pad_s[L + CONV_HALO:L + 2 * CONV_HALO, :] = jnp.zeros((CONV_HALO, CONV_W), F32)

    def conv_group(src_ref, w_ref, b_ref, col, dst_ref, dst_dtype):
        pad_s[CONV_HALO:L + CONV_HALO, :] = src_ref[:, col:col + CONV_W].astype(F32)
        w = w_ref[:, col:col + CONV_W]
        b = b_ref[:, col:col + CONV_W]

        def body(r, carry):
            base = pl.multiple_of(r * Q, Q)
            win = pad_s[pl.ds(base, ext), :]
            acc = jnp.zeros((Q, CONV_W), F32) + b
            for k in range(SSD_CONV):
                shift = (SSD_CONV // 2 - k) % ext
                sh = win if shift == 0 else pltpu.roll(win, shift, 0)
                acc = acc + sh[CONV_HALO:CONV_HALO + Q, :] * w[k:k + 1, :]
            dst_ref[pl.ds(base, Q), col:col + CONV_W] = _silu(acc).astype(dst_dtype)
            return carry

        lax.fori_loop(0, nc, body, 0)

    for cg in range(SSD_INNER // CONV_W):
        conv_group(xs_ref, cwx_ref, cbx_ref, cg * CONV_W, xs_s, F32)
    conv_group(bm_ref, cwb_ref, cbb_ref, 0, bm_s, BF16)
    conv_group(cm_ref, cwc_ref, cbc_ref, 0, cm_s, BF16)

    dt_s[...] = _softplus(sm_ref[...].astype(F32) + dtb_ref[...])

    li = lax.broadcasted_iota(jnp.int32, (Q, Q), 0)
    si = lax.broadcasted_iota(jnp.int32, (Q, Q), 1)
    lane = lax.broadcasted_iota(jnp.int32, (Q, SSD_INNER), 1)
    low_half = (lane % LANE) < SSD_HEAD_DIM
    gw = SSD_INNER // SSD_GROUPS
    heads_per_group = SSD_HEADS // SSD_GROUPS

    def chunk_step(c, d):
        rs = pl.ds(pl.multiple_of(c * Q, Q), Q)
        keep = (si <= li) if d == 0 else (si >= li)
        tri = tri_ref[d]
        dt_c = dt_s[rs, :]
        da_c = dt_c * aneg_ref[...]
        da_hi, da_lo = _split(da_c)
        ac = _dot(tri, da_hi) + _dot(tri, da_lo)
        tri_t = tri_ref[1 - d]
        ac_t = _dot(da_hi, tri_t, TN) + _dot(da_lo, tri_t, TN)
        ac_hi, ac_lo = _split(ac)
        ac_x = _dot(jnp.concatenate([ac_hi, ac_lo], axis=1), exp_ref[d])
        dt_hi, dt_lo = _split(dt_c)
        dt_x = _dot(jnp.concatenate([dt_hi, dt_lo], axis=1), exp_ref[d])
        xs_c = xs_s[rs, :]
        xd = xs_c * dt_x
        xd_b = xd.astype(BF16)
        a_last = ac_x[Q - 1:Q, :] if d == 0 else ac_x[0:1, :]
        xdd = (xd * jnp.exp(a_last - ac_x)).astype(BF16)
        eac = jnp.exp(ac_x)
        xd_lo = jnp.where(low_half, xd_b, jnp.zeros_like(xd_b))
        xd_hi = jnp.where(low_half, jnp.zeros_like(xd_b), xd_b)
        cm_c = cm_s[rs, :]
        bm_c = bm_s[rs, :]
        for g in range(SSD_GROUPS):
            cmg = cm_c[:, g * SSD_STATE:(g + 1) * SSD_STATE]
            bmg = bm_c[:, g * SSD_STATE:(g + 1) * SSD_STATE]
            cb = _dot(cmg, bmg, NT)
            s_prev = st_s[g]
            y_off = _dot(cmg, s_prev.astype(BF16)) * eac[:, g * gw:(g + 1) * gw]
            st_new = _dot(bmg, xdd[:, g * gw:(g + 1) * gw], TN)
            st_s[g] = jnp.exp(a_last[:, g * gw:(g + 1) * gw]) * s_prev + st_new
            for pair in range(heads_per_group // 2):
                c0 = g * gw + pair * LANE
                y_pair = y_off[:, pair * LANE:(pair + 1) * LANE]
                for q in range(2):
                    e = d * SSD_HEADS + g * heads_per_group + 2 * pair + q
                    seg = ac[:, e:e + 1] - ac_t[e:e + 1, :]
                    lm = jnp.where(keep, jnp.exp(seg), 0.0) * cb
                    xq = xd_lo if q == 0 else xd_hi
                    y_pair = y_pair + _dot(lm.astype(BF16), xq[:, c0:c0 + LANE])
                if d == 0:
                    yacc_s[rs, c0:c0 + LANE] = y_pair + xs_c[:, c0:c0 + LANE] * dskip_ref[:, c0:c0 + LANE]
                else:
                    o_ref[rs, c0:c0 + LANE] = (yacc_s[rs, c0:c0 + LANE] + y_pair).astype(o_ref.dtype)

    st_s[...] = jnp.zeros(st_s.shape, F32)

    def fwd(i, carry):
        chunk_step(i, 0)
        return carry

    lax.fori_loop(0, nc, fwd, 0)
    st_s[...] = jnp.zeros(st_s.shape, F32)

    def bwd(i, carry):
        chunk_step(nc - 1 - i, 1)
        return carry

    lax.fori_loop(0, nc, bwd, 0)


def _ssd(u3, conv_w, conv_b, a_log, dt_bias, d_skip):
    B, L, _ = u3.shape
    cwx, cwb, cwc = conv_w[:, :SSD_INNER], conv_w[:, SSD_INNER:SSD_INNER + SSD_BC], conv_w[:, SSD_INNER + SSD_BC:]
    cb2 = conv_b[None, :]
    cbx, cbb, cbc = cb2[:, :SSD_INNER], cb2[:, SSD_INNER:SSD_INNER + SSD_BC], cb2[:, SSD_INNER + SSD_BC:]
    zpad = jnp.zeros((LANE - 2 * SSD_HEADS,), F32)
    dtb = jnp.concatenate([dt_bias.reshape(-1), zpad])[None, :]
    aneg = jnp.concatenate([-jnp.exp(a_log.reshape(-1)), zpad])[None, :]
    dskip = jnp.repeat(d_skip, SSD_HEAD_DIM)[None, :]
    l_idx = np.arange(SSD_CHUNK)
    tri = np.stack([l_idx[None, :] <= l_idx[:, None], l_idx[None, :] >= l_idx[:, None]]).astype(np.float32)
    expm = np.zeros((2, 2 * LANE, SSD_INNER), np.float32)
    for d in range(2):
        for e in range(SSD_HEADS):
            for part in range(2):
                expm[d, part * LANE + d * SSD_HEADS + e, e * SSD_HEAD_DIM:(e + 1) * SSD_HEAD_DIM] = 1.0
    const2 = lambda b: (0, 0)
    const3 = lambda b: (0, 0, 0)
    return pl.pallas_call(
        _ssd_kernel,
        grid=(B,),
        in_specs=[
            pl.BlockSpec((None, L, SSD_INNER), lambda b: (b, 0, COL_XS // SSD_INNER)),
            pl.BlockSpec((None, L, SSD_BC), lambda b: (b, 0, COL_BM // SSD_BC)),
            pl.BlockSpec((None, L, SSD_BC), lambda b: (b, 0, COL_CM // SSD_BC)),
            pl.BlockSpec((None, L, LANE), lambda b: (b, 0, COL_SMALL // LANE)),
            pl.BlockSpec((SSD_CONV, SSD_INNER), const2),
            pl.BlockSpec((SSD_CONV, SSD_BC), const2),
            pl.BlockSpec((SSD_CONV, SSD_BC), const2),
            pl.BlockSpec((1, SSD_INNER), const2),
            pl.BlockSpec((1, SSD_BC), const2),
            pl.BlockSpec((1, SSD_BC), const2),
            pl.BlockSpec((1, LANE), const2),
            pl.BlockSpec((1, LANE), const2),
            pl.BlockSpec((1, SSD_INNER), const2),
            pl.BlockSpec((2, SSD_CHUNK, SSD_CHUNK), const3),
            pl.BlockSpec((2, 2 * LANE, SSD_INNER), const3),
        ],
        out_specs=pl.BlockSpec((None, L, SSD_INNER), lambda b: (b, 0, 0)),
        out_shape=jax.ShapeDtypeStruct((B, L, SSD_INNER), BF16),
        scratch_shapes=[
            pltpu.VMEM((L + 2 * CONV_HALO, CONV_W), F32),
            pltpu.VMEM((L, SSD_INNER), F32),
            pltpu.VMEM((L, SSD_BC), BF16),
            pltpu.VMEM((L, SSD_BC), BF16),
            pltpu.VMEM((L, LANE), F32),
            pltpu.VMEM((L, SSD_INNER), F32),
            pltpu.VMEM((SSD_GROUPS, SSD_STATE, SSD_INNER // SSD_GROUPS), F32),
        ],
        compiler_params=_params("parallel"),
        name="ssd",
    )(u3, u3, u3, u3, cwx, cwb, cwc, cbx, cbb, cbc, dtb, aneg, dskip,
      jnp.asarray(tri, BF16), jnp.asarray(expm, BF16))


def _attend(q_blk, k, v):
    s = _dot(q_blk, k, NT)
    m = jnp.max(s, axis=-1, keepdims=True)
    p = jnp.exp(s - m)
    denom = jnp.sum(p, axis=-1, keepdims=True)
    return _dot(p.astype(BF16), v) * (1.0 / denom)


def _rope(x, cos, sin_a, sin_b, half):
    w = x.shape[-1]
    return x * cos + pltpu.roll(x, w - half, 1) * sin_a + pltpu.roll(x, half, 1) * sin_b


def _attn_tq(L):
    return 512 if L % 512 == 0 else L


MLA_PAIR = 2


def _mla_kernel(ql_ref, kv_ref, gq_ref, gkv_ref, wq_ref, wk_ref, wv_ref, cosq_ref, cosk_ref, sa_ref, sb_ref,
                o_ref, qn_s, kvn_s, kr_s, q_s, k_s, v_s):
    L = ql_ref.shape[0]
    tq = _attn_tq(L)
    half = MLA_ROPE // 4

    @pl.when(pl.program_id(1) == 0)
    def _():
        ql = ql_ref[:, 0:MLA_Q_RANK].astype(F32)
        ms = jnp.mean(ql * ql, axis=-1, keepdims=True)
        qn_s[...] = (ql * lax.rsqrt(ms + NORM_EPS) * gq_ref[...]).astype(BF16)
        kv = kv_ref[...].astype(F32)
        ms = jnp.mean(kv * kv, axis=-1, keepdims=True)
        kvn_s[...] = (kv * lax.rsqrt(ms + NORM_EPS) * gkv_ref[...]).astype(BF16)
        sm = ql_ref[:, MLA_Q_RANK:MLA_Q_RANK + LANE].astype(F32)
        kr_s[...] = _rope(sm, cosk_ref[...], sa_ref[...], sb_ref[...], half)

    scale = (MLA_NOPE + MLA_ROPE) ** -0.5
    for h in range(MLA_PAIR):
        q = _dot(qn_s[...], wq_ref[h])
        q = _rope(q, cosq_ref[...], sa_ref[...], sb_ref[...], half) * scale
        q_s[h] = q.astype(BF16)
        k_s[h] = (_dot(kvn_s[...], wk_ref[h]) + kr_s[...]).astype(BF16)
        v_s[h] = _dot(kvn_s[...], wv_ref[h]).astype(BF16)

    def body(i, carry):
        rs = pl.ds(pl.multiple_of(i * tq, tq), tq)
        o = _attend(q_s[0, rs, :], k_s[0], v_s[0])
        for h in range(1, MLA_PAIR):
            o = o + _attend(q_s[h, rs, :], k_s[h], v_s[h])
        o_ref[rs, :] = o.astype(o_ref.dtype)
        return carry

    lax.fori_loop(0, L // tq, body, 0)


def _rope_tables(L, d_rot, width, lane0):
    rows = L // GRID_W
    row = jnp.repeat(jnp.arange(rows), GRID_W).astype(F32)
    col = jnp.tile(jnp.arange(GRID_W), rows).astype(F32)
    m = d_rot // 2
    inv = ROPE_THETA ** (-jnp.arange(0, m, 2, dtype=F32) / m)
    ang_r = row[:, None] * inv
    ang_c = col[:, None] * inv
    ang = jnp.concatenate([ang_r, ang_r, ang_c, ang_c], axis=-1)
    cos, sin = jnp.cos(ang), jnp.sin(ang)
    first = (jnp.arange(d_rot) % m) < (m // 2)
    sin_a = jnp.where(first, -sin, 0.0)
    sin_b = jnp.where(first, 0.0, sin)
    if lane0 is None:
        reps = width // d_rot
        return tuple(jnp.tile(t, (1, reps)) for t in (cos, sin_a, sin_b))
    padl, padr = lane0, width - lane0 - d_rot
    return tuple(jnp.pad(t, ((0, 0), (padl, padr))) for t in (cos, sin_a, sin_b))


def _mla(u3, gq, gkv, w_q_b, w_kv_b):
    B, L, _ = u3.shape
    npair = MLA_HEADS // MLA_PAIR
    dq = MLA_NOPE + MLA_ROPE
    wq = w_q_b.reshape(MLA_Q_RANK, MLA_HEADS, dq).transpose(1, 0, 2)
    wq = jnp.pad(wq, ((0, 0), (0, 0), (0, LANE - dq))).astype(BF16)
    wkv = w_kv_b.reshape(MLA_KV_RANK, MLA_HEADS, MLA_NOPE + MLA_V).transpose(1, 0, 2)
    wk = jnp.pad(wkv[..., :MLA_NOPE], ((0, 0), (0, 0), (0, LANE - MLA_NOPE))).astype(BF16)
    wv_h = wkv[..., MLA_NOPE:]
    slot = (jnp.arange(MLA_HEADS) % MLA_PAIR)[:, None, None, None]
    wv = jnp.where(slot == jnp.arange(MLA_PAIR)[None, None, :, None], wv_h[:, :, None, :], 0.0)
    wv = wv.reshape(MLA_HEADS, MLA_KV_RANK, MLA_PAIR * MLA_V).astype(BF16)
    cos, sin_a, sin_b = _rope_tables(L, MLA_ROPE, LANE, SMALL_KROPE)
    cosq = cos.at[:, :MLA_NOPE].set(1.0)
    const2 = lambda b, p: (0, 0)
    return pl.pallas_call(
        _mla_kernel,
        grid=(B, npair),
        in_specs=[
            pl.BlockSpec((None, L, 512), lambda b, p: (b, 0, COL_QLAT // 512)),
            pl.BlockSpec((None, L, MLA_KV_RANK), lambda b, p: (b, 0, COL_KVLAT // MLA_KV_RANK)),
            pl.BlockSpec((1, MLA_Q_RANK), const2),
            pl.BlockSpec((1, MLA_KV_RANK), const2),
            pl.BlockSpec((MLA_PAIR, MLA_Q_RANK, LANE), lambda b, p: (p, 0, 0)),
            pl.BlockSpec((MLA_PAIR, MLA_KV_RANK, LANE), lambda b, p: (p, 0, 0)),
            pl.BlockSpec((MLA_PAIR, MLA_KV_RANK, LANE), lambda b, p: (p, 0, 0)),
            pl.BlockSpec((L, LANE), const2),
            pl.BlockSpec((L, LANE), const2),
            pl.BlockSpec((L, LANE), const2),
            pl.BlockSpec((L, LANE), const2),
        ],
        out_specs=pl.BlockSpec((None, L, LANE), lambda b, p: (b, 0, p)),
        out_shape=jax.ShapeDtypeStruct((B, L, MLA_WIDTH), BF16),
        scratch_shapes=[
            pltpu.VMEM((L, MLA_Q_RANK), BF16),
            pltpu.VMEM((L, MLA_KV_RANK), BF16),
            pltpu.VMEM((L, LANE), F32),
            pltpu.VMEM((MLA_PAIR, L, LANE), BF16),
            pltpu.VMEM((MLA_PAIR, L, LANE), BF16),
            pltpu.VMEM((MLA_PAIR, L, LANE), BF16),
        ],
        compiler_params=_params("parallel", "arbitrary"),
        name="mla",
    )(u3, u3, gq[None, :], gkv[None, :], wq, wk, wv, cosq, cos, sin_a, sin_b)


def _gqa_kernel(q_ref, kv_ref, gq_ref, gk_ref, bd_ref, rep_ref, cos_ref, sa_ref, sb_ref, o_ref,
                q_s, k_s, v_s):
    L = q_ref.shape[0]
    tq = _attn_tq(L)
    half = GQA_HEAD_DIM // 4
    kw = GQA_KV_HEADS * GQA_HEAD_DIM

    def headnorm(x, bd, gain):
        ms = _dot2(x * x, bd) * (1.0 / GQA_HEAD_DIM)
        return x * lax.rsqrt(ms + NORM_EPS) * gain

    q = headnorm(q_ref[...].astype(F32), bd_ref[...], gq_ref[...])
    q = _rope(q, cos_ref[...], sa_ref[...], sb_ref[...], half) * (GQA_HEAD_DIM ** -0.5)
    q_s[...] = q.astype(BF16)
    k = headnorm(kv_ref[:, 0:kw].astype(F32), bd_ref[0:kw, 0:kw], gk_ref[...])
    k = _rope(k, cos_ref[:, 0:kw], sa_ref[:, 0:kw], sb_ref[:, 0:kw], half)
    k_s[...] = _dot(k.astype(BF16), rep_ref[...]).astype(BF16)
    v_rep = _dot(kv_ref[:, kw:2 * kw], rep_ref[...]).astype(BF16)
    lane = lax.broadcasted_iota(jnp.int32, (L, GQA_GROUP_W), 1)
    for h in range(GQA_REP):
        v_s[h] = jnp.where(lane // GQA_HEAD_DIM == h, v_rep, jnp.zeros_like(v_rep))

    lane_q = lax.broadcasted_iota(jnp.int32, (tq, GQA_GROUP_W), 1)

    def body(i, carry):
        rs = pl.ds(pl.multiple_of(i * tq, tq), tq)
        qb = q_s[rs, :]
        o = jnp.zeros((tq, GQA_GROUP_W), F32)
        for h in range(GQA_REP):
            qh = jnp.where(lane_q // GQA_HEAD_DIM == h, qb, jnp.zeros_like(qb))
            o = o + _attend(qh, k_s[...], v_s[h])
        o_ref[rs, :] = o.astype(o_ref.dtype)
        return carry

    lax.fori_loop(0, L // tq, body, 0)


def _gqa(u3, q_norm_g, k_norm_g):
    B, L, _ = u3.shape
    kw = GQA_KV_HEADS * GQA_HEAD_DIM
    gq = jnp.tile(q_norm_g, GQA_REP)[None, :]
    gk = jnp.tile(k_norm_g, GQA_KV_HEADS)[None, :]
    idx = np.arange(GQA_GROUP_W)
    bd = (idx[:, None] // GQA_HEAD_DIM == idx[None, :] // GQA_HEAD_DIM).astype(np.float32)
    rep = np.zeros((GQA_KV_HEADS, kw, GQA_GROUP_W), np.float32)
    for g in range(GQA_KV_HEADS):
        rep[g, g * GQA_HEAD_DIM + idx % GQA_HEAD_DIM, idx] = 1.0
    cos, sin_a, sin_b = _rope_tables(L, GQA_HEAD_DIM, GQA_GROUP_W, None)
    const2 = lambda b, g: (0, 0)
    return pl.pallas_call(
        _gqa_kernel,
        grid=(B, GQA_KV_HEADS),
        in_specs=[
            pl.BlockSpec((None, L, GQA_GROUP_W), lambda b, g: (b, 0, COL_QD // GQA_GROUP_W + g)),
            pl.BlockSpec((None, L, 2 * kw), lambda b, g: (b, 0, COL_KD // (2 * kw))),
            pl.BlockSpec((1, GQA_GROUP_W), const2),
            pl.BlockSpec((1, kw), const2),
            pl.BlockSpec((GQA_GROUP_W, GQA_GROUP_W), const2),
            pl.BlockSpec((None, kw, GQA_GROUP_W), lambda b, g: (g, 0, 0)),
            pl.BlockSpec((L, GQA_GROUP_W), const2),
            pl.BlockSpec((L, GQA_GROUP_W), const2),
            pl.BlockSpec((L, GQA_GROUP_W), const2),
        ],
        out_specs=pl.BlockSpec((None, L, GQA_GROUP_W), lambda b, g: (b, 0, g)),
        out_shape=jax.ShapeDtypeStruct((B, L, GQA_WIDTH), BF16),
        scratch_shapes=[
            pltpu.VMEM((L, GQA_GROUP_W), BF16),
            pltpu.VMEM((L, GQA_GROUP_W), BF16),
            pltpu.VMEM((GQA_REP, L, GQA_GROUP_W), BF16),
        ],
        compiler_params=_params("parallel", "parallel"),
        name="gqa",
    )(u3, u3, gq, gk, jnp.asarray(bd, BF16), jnp.asarray(rep, BF16), cos, sin_a, sin_b)


def _gla_kernel(q_ref, k_ref, v_ref, sm_ref, wg_ref, bg_ref, tri_ref, ones_ref, o_ref,
                glog_s, oacc_s, st_s):
    L = q_ref.shape[0]
    Q = GLA_CHUNK
    nc = L // Q

    sm = sm_ref[...]
    g_pre = _dot(jnp.concatenate([sm, sm], axis=1), wg_ref[...]) + bg_ref[...]
    glog_s[...] = -_softplus(-g_pre) * (1.0 / GLA_TAU)

    li = lax.broadcasted_iota(jnp.int32, (Q, Q), 0)
    si = lax.broadcasted_iota(jnp.int32, (Q, Q), 1)
    lane = lax.broadcasted_iota(jnp.int32, (Q, GLA_QK), 1)

    def chunk_step(c, d):
        rs = pl.ds(pl.multiple_of(c * Q, Q), Q)
        keep = (si <= li) if d == 0 else (si >= li)
        g_c = glog_s[rs, d * GLA_QK:(d + 1) * GLA_QK]
        g_hi, g_lo = _split(g_c)
        gcum = _dot(tri_ref[d], g_hi) + _dot(tri_ref[d], g_lo)
        g_last = gcum[Q - 1:Q, :] if d == 0 else gcum[0:1, :]
        q = q_ref[rs, :].astype(F32) * (GLA_DK ** -0.5)
        k = k_ref[rs, :].astype(F32)
        v = v_ref[rs, :]
        qg = (q * jnp.exp(gcum)).astype(BF16)
        kg = (k * jnp.exp(-gcum)).astype(BF16)
        k_end = (k * jnp.exp(g_last - gcum)).astype(BF16)
        u = _dot(k_end, v, TN)
        g_last_t = _dot(g_hi, ones_ref[...], TN) + _dot(g_lo, ones_ref[...], TN)
        dec = jnp.exp(g_last_t)
        s_prev = st_s[...]
        s_prev_b = s_prev.astype(BF16)
        for h in range(GLA_HEADS):
            qh = jnp.where(lane // GLA_DK == h, qg, jnp.zeros_like(qg))
            att = jnp.where(keep, _dot(qh, kg, NT), 0.0)
            vs = slice(h * GLA_DV, (h + 1) * GLA_DV)
            o_h = _dot(att.astype(BF16), v[:, vs]) + _dot(qh, s_prev_b)
            if d == 0:
                oacc_s[rs, vs] = o_h
            else:
                o_ref[rs, vs] = (oacc_s[rs, vs] + o_h).astype(o_ref.dtype)
            ks = slice(h * GLA_DK, (h + 1) * GLA_DK)
            st_s[ks, :] = dec[ks, :] * s_prev[ks, :] + u[ks, vs]

    st_s[...] = jnp.zeros(st_s.shape, F32)

    def fwd(i, carry):
        chunk_step(i, 0)
        return carry

    lax.fori_loop(0, nc, fwd, 0)
    st_s[...] = jnp.zeros(st_s.shape, F32)

    def bwd(i, carry):
        chunk_step(nc - 1 - i, 1)
        return carry

    lax.fori_loop(0, nc, bwd, 0)


def _gla(u3, w_gate_up, b_gate):
    B, L, _ = u3.shape
    wg = jnp.zeros((LANE, 2 * GLA_QK), F32)
    for d in range(2):
        r0 = SMALL_GLR + d * GLA_GATE_RANK
        wg = wg.at[r0:r0 + GLA_GATE_RANK, d * GLA_QK:(d + 1) * GLA_QK].set(w_gate_up[d])
    wg_hi, wg_lo = _split(wg)
    wg2 = jnp.concatenate([wg_hi, wg_lo], axis=0)
    bg = b_gate.reshape(1, 2 * GLA_QK)
    l_idx = np.arange(GLA_CHUNK)
    tri = np.stack([l_idx[None, :] <= l_idx[:, None], l_idx[None, :] >= l_idx[:, None]]).astype(np.float32)
    ones = np.ones((GLA_CHUNK, LANE), np.float32)
    const2 = lambda b: (0, 0)
    return pl.pallas_call(
        _gla_kernel,
        grid=(B,),
        in_specs=[
            pl.BlockSpec((None, L, GLA_QK), lambda b: (b, 0, COL_QC // GLA_QK)),
            pl.BlockSpec((None, L, GLA_QK), lambda b: (b, 0, COL_KC // GLA_QK)),
            pl.BlockSpec((None, L, GLA_WIDTH), lambda b: (b, 0, COL_VC // GLA_WIDTH)),
            pl.BlockSpec((None, L, LANE), lambda b: (b, 0, COL_SMALL // LANE)),
            pl.BlockSpec((2 * LANE, 2 * GLA_QK), const2),
            pl.BlockSpec((1, 2 * GLA_QK), const2),
            pl.BlockSpec((2, GLA_CHUNK, GLA_CHUNK), lambda b: (0, 0, 0)),
            pl.BlockSpec((GLA_CHUNK, LANE), const2),
        ],
        out_specs=pl.BlockSpec((None, L, GLA_WIDTH), lambda b: (b, 0, 0)),
        out_shape=jax.ShapeDtypeStruct((B, L, GLA_WIDTH), BF16),
        scratch_shapes=[
            pltpu.VMEM((L, 2 * GLA_QK), F32),
            pltpu.VMEM((L, GLA_WIDTH), F32),
            pltpu.VMEM((GLA_QK, GLA_DV), F32),
        ],
        compiler_params=_params("parallel"),
        name="gla",
    )(u3, u3, u3, u3, wg2, bg, jnp.asarray(tri, BF16), jnp.asarray(ones, BF16))


def _merge_kernel(gm_ref, za_ref, zb_ref, zc_ref, zd_ref, ya_ref, ob_ref, oc_ref, od_ref, x_ref,
                  wa_ref, wb_ref, wc_ref, wd_ref, wo_ref, ga_ref, gc_ref, fg_ref, o_ref, *, final_norm):
    def gated(o_ref_, z_ref_):
        return o_ref_[...].astype(F32) * _silu(z_ref_[...].astype(F32))

    ya = gated(ya_ref, za_ref)
    ms = jnp.mean(ya * ya, axis=-1, keepdims=True)
    ya = (ya * lax.rsqrt(ms + NORM_EPS) * ga_ref[...]).astype(BF16)
    mixed = _sigmoid(gm_ref[:, 0:D_MODEL].astype(F32)) * _dot(ya, wa_ref[...])

    yb = gated(ob_ref, zb_ref).astype(BF16)
    mixed = mixed + _sigmoid(gm_ref[:, D_MODEL:2 * D_MODEL].astype(F32)) * _dot(yb, wb_ref[...])

    oc = oc_ref[...].astype(F32)
    parts = []
    for h in range(GLA_HEADS):
        oh = oc[:, h * GLA_DV:(h + 1) * GLA_DV]
        ms = jnp.mean(oh * oh, axis=-1, keepdims=True)
        parts.append(oh * lax.rsqrt(ms + NORM_EPS))
    yc = jnp.concatenate(parts, axis=1) * gc_ref[...] * _silu(zc_ref[...].astype(F32))
    mixed = mixed + _sigmoid(gm_ref[:, 2 * D_MODEL:3 * D_MODEL].astype(F32)) * _dot(yc.astype(BF16), wc_ref[...])

    yd = gated(od_ref, zd_ref).astype(BF16)
    mixed = mixed + _sigmoid(gm_ref[:, 3 * D_MODEL:4 * D_MODEL].astype(F32)) * _dot(yd, wd_ref[...])

    x = x_ref[...] + _dot(mixed.astype(BF16), wo_ref[...])
    if final_norm:
        ms = jnp.mean(x * x, axis=-1, keepdims=True)
        x = x * lax.rsqrt(ms + NORM_EPS) * fg_ref[...]
    o_ref[...] = x


def _merge(u2, ya, ob, oc, od, x2, wa, wb, wc, wd, wo, ga, gc, fg, final_norm):
    t = x2.shape[0]
    tm = min(256, t)
    row = lambda c: (lambda i: (i, c))
    const2 = lambda i: (0, 0)
    return pl.pallas_call(
        functools.partial(_merge_kernel, final_norm=final_norm),
        grid=(t // tm,),
        in_specs=[
            pl.BlockSpec((tm, N_BRANCH * D_MODEL), row(COL_GMERGE // (N_BRANCH * D_MODEL))),
            pl.BlockSpec((tm, SSD_INNER), row(COL_ZA // SSD_INNER)),
            pl.BlockSpec((tm, MLA_WIDTH), row(COL_ZB // MLA_WIDTH)),
            pl.BlockSpec((tm, GLA_WIDTH), row(COL_ZC // GLA_WIDTH)),
            pl.BlockSpec((tm, GQA_WIDTH), row(COL_ZD // GQA_WIDTH)),
            pl.BlockSpec((tm, SSD_INNER), row(0)),
            pl.BlockSpec((tm, MLA_WIDTH), row(0)),
            pl.BlockSpec((tm, GLA_WIDTH), row(0)),
            pl.BlockSpec((tm, GQA_WIDTH), row(0)),
            pl.BlockSpec((tm, D_MODEL), row(0)),
            pl.BlockSpec((SSD_INNER, D_MODEL), const2),
            pl.BlockSpec((MLA_WIDTH, D_MODEL), const2),
            pl.BlockSpec((GLA_WIDTH, D_MODEL), const2),
            pl.BlockSpec((GQA_WIDTH, D_MODEL), const2),
            pl.BlockSpec((D_MODEL, D_MODEL), const2),
            pl.BlockSpec((1, SSD_INNER), const2),
            pl.BlockSpec((1, GLA_WIDTH), const2),
            pl.BlockSpec((1, D_MODEL), const2),
        ],
        out_specs=pl.BlockSpec((tm, D_MODEL), row(0)),
        out_shape=jax.ShapeDtypeStruct((t, D_MODEL), F32),
        compiler_params=_params("parallel"),
        name="merge",
    )(u2, u2, u2, u2, u2, ya, ob, oc, od, x2, wa, wb, wc, wd, wo, ga, gc, fg)


def _reorder_w_in(w):
    split_idx = [int(v) for v in np.cumsum(IN_WIDTHS)[:-1]]
    (g_merge, z_a, xbc, dt_raw, z_b, q_lat, kv_lat, k_rope, z_c, q_c, k_c, v_c, g_lr,
     z_d, q_d, k_d, v_d) = jnp.split(w, split_idx, axis=1)
    xs, bm, cm = xbc[:, :SSD_INNER], xbc[:, SSD_INNER:SSD_INNER + SSD_BC], xbc[:, SSD_INNER + SSD_BC:]
    small = jnp.concatenate([dt_raw, g_lr, k_rope, jnp.zeros((w.shape[0], LANE - 96), w.dtype)], axis=1)
    cols = [g_merge, z_a, xs, z_b, z_c, z_d, q_d, v_c, q_lat, small, bm, cm, kv_lat, q_c, k_c, k_d, v_d]
    out = jnp.concatenate(cols, axis=1)
    assert out.shape[1] == N_PROJ
    return out.astype(BF16)


def kernel(x, norm_g, w_in, conv_w, conv_b, a_log, dt_bias, d_skip, ssd_norm_g, q_lat_norm_g, kv_lat_norm_g, w_q_b, w_kv_b, w_gate_up, b_gate, gla_norm_g, q_norm_g, k_norm_g, w_br_a, w_br_b, w_br_c, w_br_d, w_out, final_g):
    B, L, D = x.shape
    depth = w_in.shape[0]
    x2 = x.reshape(B * L, D)
    for i in range(depth):
        u2 = _inproj(x2, norm_g[i][None, :], _reorder_w_in(w_in[i]))
        u3 = u2.reshape(B, L, N_PROJ)
        ya = _ssd(u3, conv_w[i], conv_b[i], a_log[i], dt_bias[i], d_skip[i])
        ob = _mla(u3, q_lat_norm_g[i], kv_lat_norm_g[i], w_q_b[i], w_kv_b[i])
        oc = _gla(u3, w_gate_up[i], b_gate[i])
        od = _gqa(u3, q_norm_g[i], k_norm_g[i])
        x2 = _merge(
            u2, ya.reshape(B * L, -1), ob.reshape(B * L, -1), oc.reshape(B * L, -1), od.reshape(B * L, -1), x2,
            w_br_a[i].astype(BF16), w_br_b[i].astype(BF16), w_br_c[i].astype(BF16), w_br_d[i].astype(BF16),
            w_out[i].astype(BF16), ssd_norm_g[i][None, :], gla_norm_g[i][None, :], final_g[None, :],
            final_norm=(i == depth - 1))
    return x2.reshape(B, L, D)
```

```python
import functools

import numpy as np
import jax
import jax.numpy as jnp
from jax import lax
from jax.experimental import pallas as pl
from jax.experimental.pallas import tpu as pltpu

F32 = jnp.float32
BF16 = jnp.bfloat16

D_MODEL = 1024
GRID_W = 64
ROPE_THETA = 10000.0
NORM_EPS = 1e-6
N_BRANCH = 4

SSD_HEADS = 16
SSD_HEAD_DIM = 64
SSD_INNER = SSD_HEADS * SSD_HEAD_DIM
SSD_GROUPS = 2
SSD_STATE = 128
SSD_CONV = 5
SSD_CHUNK = 128
SSD_BC = SSD_GROUPS * SSD_STATE

MLA_HEADS = 8
MLA_Q_RANK = 384
MLA_KV_RANK = 256
MLA_NOPE = 64
MLA_ROPE = 32
MLA_V = 64
MLA_WIDTH = MLA_HEADS * MLA_V

GLA_HEADS = 4
GLA_DK = 64
GLA_DV = 128
GLA_GATE_RANK = 16
GLA_TAU = 16.0
GLA_CHUNK = 64
GLA_WIDTH = GLA_HEADS * GLA_DV
GLA_QK = GLA_HEADS * GLA_DK

GQA_HEADS = 8
GQA_KV_HEADS = 2
GQA_HEAD_DIM = 64
GQA_WIDTH = GQA_HEADS * GQA_HEAD_DIM
GQA_REP = GQA_HEADS // GQA_KV_HEADS
GQA_GROUP_W = GQA_REP * GQA_HEAD_DIM

IN_WIDTHS = (
    N_BRANCH * D_MODEL, SSD_INNER, SSD_INNER + 2 * SSD_BC, 2 * SSD_HEADS, MLA_WIDTH, MLA_Q_RANK,
    MLA_KV_RANK, MLA_ROPE, GLA_WIDTH, GLA_QK, GLA_QK, GLA_WIDTH, 2 * GLA_GATE_RANK, GQA_WIDTH,
    GQA_WIDTH, GQA_KV_HEADS * GQA_HEAD_DIM, GQA_KV_HEADS * GQA_HEAD_DIM,
)

LANE = 128

SMALL_DT = 0
SMALL_GLR = 32
SMALL_KROPE = 64
COL_GMERGE = 0
COL_ZA = 4096
COL_XS = 5120
COL_ZB = 6144
COL_ZC = 6656
COL_ZD = 7168
COL_QD = 7680
COL_VC = 8192
COL_QLAT = 8704
COL_SMALL = COL_QLAT + MLA_Q_RANK
COL_BM = 9216
COL_CM = 9472
COL_KVLAT = 9728
COL_QC = 9984
COL_KC = 10240
COL_KD = 10496
N_PROJ = 10752

VMEM_LIMIT = 56 * 1024 * 1024

NN = ((1,), (0,))
NT = ((1,), (1,))
TN = ((0,), (0,))


def _dot(a, b, dims=NN):
    return lax.dot_general(a, b, (dims, ((), ())), preferred_element_type=F32)


def _split(a):
    hi = a.astype(BF16)
    lo = (a - hi.astype(F32)).astype(BF16)
    return hi, lo


def _dot2(a, m, dims=NN):
    hi, lo = _split(a)
    return _dot(hi, m, dims) + _dot(lo, m, dims)


def _dot2r(m, a, dims=NN):
    hi, lo = _split(a)
    return _dot(m, hi, dims) + _dot(m, lo, dims)


def _softplus(x):
    return jnp.maximum(x, 0.0) + jnp.log1p(jnp.exp(-jnp.abs(x)))


def _silu(x):
    return x * (1.0 / (1.0 + jnp.exp(-x)))


def _sigmoid(x):
    return 1.0 / (1.0 + jnp.exp(-x))


def _params(*sem):
    return pltpu.CompilerParams(dimension_semantics=sem, vmem_limit_bytes=VMEM_LIMIT)


def _interleave(gens):
    gens = list(gens)
    while gens:
        alive = []
        for g in gens:
            try:
                next(g)
                alive.append(g)
            except StopIteration:
                pass
        gens = alive


def _both_directions(nc, chunk_step, unroll=1):
    assert nc % 2 == 0
    half = nc // 2
    if half % unroll:
        unroll = 1

    def make(first, base):
        def body(i, carry):
            chains = []
            for u in range(unroll):
                c = base + i * unroll + u
                chains += [chunk_step(c, 0, first), chunk_step(nc - 1 - c, 1, first)]
            _interleave(chains)
            return carry
        return body

    lax.fori_loop(0, half // unroll, make(True, 0), 0)
    lax.fori_loop(0, half // unroll, make(False, half), 0)


def _inproj_kernel(x_ref, g_ref, w_ref, o_ref, h_ref):
    @pl.when(pl.program_id(1) == 0)
    def _():
        x = x_ref[...]
        ms = jnp.mean(x * x, axis=-1, keepdims=True)
        h_ref[...] = (x * lax.rsqrt(ms + NORM_EPS) * g_ref[...]).astype(BF16)

    o_ref[...] = _dot(h_ref[...], w_ref[...]).astype(BF16)


def _inproj(x2, g, w):
    t = x2.shape[0]
    tm = min(1024, t)
    tn = 1536
    return pl.pallas_call(
        _inproj_kernel,
        grid=(t // tm, N_PROJ // tn),
        in_specs=[
            pl.BlockSpec((tm, D_MODEL), lambda i, j: (i, 0)),
            pl.BlockSpec((1, D_MODEL), lambda i, j: (0, 0)),
            pl.BlockSpec((D_MODEL, tn), lambda i, j: (0, j)),
        ],
        out_specs=pl.BlockSpec((tm, tn), lambda i, j: (i, j)),
        out_shape=jax.ShapeDtypeStruct((t, N_PROJ), BF16),
        scratch_shapes=[pltpu.VMEM((tm, D_MODEL), BF16)],
        compiler_params=_params("parallel", "arbitrary"),
        name="inproj",
    )(x2, g, w)


CONV_W = 256
CONV_HALO = 8


def _ssd_kernel(xs_ref, bm_ref, cm_ref, sm_ref, cwx_ref, cwb_ref, cwc_ref, cbx_ref, cbb_ref, cbc_ref,
                dtb_ref, aneg_ref, dskip_ref, tri_ref, exp_ref, o_ref,
                pad_s, xs_s, bm_s, cm_s, dt_s, yacc_s, st_s):
    L = xs_ref.shape[0]
    Q = SSD_CHUNK
    nc = L // Q
    ext = Q + 2 * CONV_HALO

    pad_s[0:CONV_HALO, :] = jnp.zeros((CONV_HALO, CONV_W), F32)
    pad_s[L + CONV_HALO:L + 2 * CONV_HALO, :] = jnp.zeros((CONV_HALO, CONV_W), F32)

    def conv_group(src_ref, w_ref, b_ref, col, dst_ref, dst_dtype):
        pad_s[CONV_HALO:L + CONV_HALO, :] = src_ref[:, col:col + CONV_W].astype(F32)
        w = w_ref[:, col:col + CONV_W]
        b = b_ref[:, col:col + CONV_W]

        def body(r, carry):
            base = pl.multiple_of(r * Q, Q)
            win = pad_s[pl.ds(base, ext), :]
            acc = jnp.zeros((Q, CONV_W), F32) + b
            for k in range(SSD_CONV):
                shift = (SSD_CONV // 2 - k) % ext
                sh = win if shift == 0 else pltpu.roll(win, shift, 0)
                acc = acc + sh[CONV_HALO:CONV_HALO + Q, :] * w[k:k + 1, :]
            dst_ref[pl.ds(base, Q), col:col + CONV_W] = _silu(acc).astype(dst_dtype)
            return carry

        lax.fori_loop(0, nc, body, 0)

    for cg in range(SSD_INNER // CONV_W):
        conv_group(xs_ref, cwx_ref, cbx_ref, cg * CONV_W, xs_s, F32)
    conv_group(bm_ref, cwb_ref, cbb_ref, 0, bm_s, BF16)
    conv_group(cm_ref, cwc_ref, cbc_ref, 0, cm_s, BF16)

    dt_s[...] = _softplus(sm_ref[...].astype(F32) + dtb_ref[...])

    li = lax.broadcasted_iota(jnp.int32, (Q, Q), 0)
    si = lax.broadcasted_iota(jnp.int32, (Q, Q), 1)
    lane = lax.broadcasted_iota(jnp.int32, (Q, SSD_INNER), 1)
    low_half = (lane % LANE) < SSD_HEAD_DIM
    gw = SSD_INNER // SSD_GROUPS
    heads_per_group = SSD_HEADS // SSD_GROUPS

    def chunk_step(c, d, first):
        rs = pl.ds(pl.multiple_of(c * Q, Q), Q)
        keep = (si <= li) if d == 0 else (si >= li)
        tri = tri_ref[d]
        dt_c = dt_s[rs, :]
        da_c = dt_c * aneg_ref[...]
        da_hi, da_lo = _split(da_c)
        ac = _dot(tri, da_hi) + _dot(tri, da_lo)
        tri_t = tri_ref[1 - d]
        ac_t = _dot(da_hi, tri_t, TN) + _dot(da_lo, tri_t, TN)
        yield
        ac_hi, ac_lo = _split(ac)
        ac_x = _dot(jnp.concatenate([ac_hi, ac_lo], axis=1), exp_ref[d])
        dt_hi, dt_lo = _split(dt_c)
        dt_x = _dot(jnp.concatenate([dt_hi, dt_lo], axis=1), exp_ref[d])
        cm_c = cm_s[rs, :]
        bm_c = bm_s[rs, :]
        cmgs = [cm_c[:, g * SSD_STATE:(g + 1) * SSD_STATE] for g in range(SSD_GROUPS)]
        bmgs = [bm_c[:, g * SSD_STATE:(g + 1) * SSD_STATE] for g in range(SSD_GROUPS)]
        cbs = [_dot(cmg, bmg, NT) for cmg, bmg in zip(cmgs, bmgs)]
        yield
        xs_c = xs_s[rs, :]
        xd = xs_c * dt_x
        xd_b = xd.astype(BF16)
        a_last = ac_x[Q - 1:Q, :] if d == 0 else ac_x[0:1, :]
        xdd = (xd * jnp.exp(a_last - ac_x)).astype(BF16)
        eac = jnp.exp(ac_x)
        xd_lo = jnp.where(low_half, xd_b, jnp.zeros_like(xd_b))
        xd_hi = jnp.where(low_half, jnp.zeros_like(xd_b), xd_b)
        st_news = [_dot(bmgs[g], xdd[:, g * gw:(g + 1) * gw], TN) for g in range(SSD_GROUPS)]
        yield
        y_diag = {}
        for g in range(SSD_GROUPS):
            for pair in range(heads_per_group // 2):
                c0 = g * gw + pair * LANE
                lms = []
                for q in range(2):
                    e = d * SSD_HEADS + g * heads_per_group + 2 * pair + q
                    seg = ac[:, e:e + 1] - ac_t[e:e + 1, :]
                    lms.append((jnp.where(keep, jnp.exp(seg), 0.0) * cbs[g]).astype(BF16))
                x_pair = jnp.concatenate([xd_lo[:, c0:c0 + LANE], xd_hi[:, c0:c0 + LANE]], axis=0)
                y = _dot(jnp.concatenate(lms, axis=1), x_pair)
                if d == 0:
                    y = y + xs_c[:, c0:c0 + LANE] * dskip_ref[:, c0:c0 + LANE]
                y_diag[c0] = y
            yield
        for g in range(SSD_GROUPS):
            s_prev = st_s[d, g]
            y_off = _dot(cmgs[g], s_prev.astype(BF16)) * eac[:, g * gw:(g + 1) * gw]
            st_s[d, g] = jnp.exp(a_last[:, g * gw:(g + 1) * gw]) * s_prev + st_news[g]
            for pair in range(heads_per_group // 2):
                c0 = g * gw + pair * LANE
                y_pair = y_diag[c0] + y_off[:, pair * LANE:(pair + 1) * LANE]
                if first:
                    yacc_s[rs, c0:c0 + LANE] = y_pair
                else:
                    o_ref[rs, c0:c0 + LANE] = (yacc_s[rs, c0:c0 + LANE] + y_pair).astype(o_ref.dtype)

    st_s[...] = jnp.zeros(st_s.shape, F32)
    _both_directions(nc, chunk_step)


def _ssd(u3, conv_w, conv_b, a_log, dt_bias, d_skip):
    B, L, _ = u3.shape
    cwx, cwb, cwc = conv_w[:, :SSD_INNER], conv_w[:, SSD_INNER:SSD_INNER + SSD_BC], conv_w[:, SSD_INNER + SSD_BC:]
    cb2 = conv_b[None, :]
    cbx, cbb, cbc = cb2[:, :SSD_INNER], cb2[:, SSD_INNER:SSD_INNER + SSD_BC], cb2[:, SSD_INNER + SSD_BC:]
    zpad = jnp.zeros((LANE - 2 * SSD_HEADS,), F32)
    dtb = jnp.concatenate([dt_bias.reshape(-1), zpad])[None, :]
    aneg = jnp.concatenate([-jnp.exp(a_log.reshape(-1)), zpad])[None, :]
    dskip = jnp.repeat(d_skip, SSD_HEAD_DIM)[None, :]
    l_idx = np.arange(SSD_CHUNK)
    tri = np.stack([l_idx[None, :] <= l_idx[:, None], l_idx[None, :] >= l_idx[:, None]]).astype(np.float32)
    expm = np.zeros((2, 2 * LANE, SSD_INNER), np.float32)
    for d in range(2):
        for e in range(SSD_HEADS):
            for part in range(2):
                expm[d, part * LANE + d * SSD_HEADS + e, e * SSD_HEAD_DIM:(e + 1) * SSD_HEAD_DIM] = 1.0
    const2 = lambda b: (0, 0)
    const3 = lambda b: (0, 0, 0)
    return pl.pallas_call(
        _ssd_kernel,
        grid=(B,),
        in_specs=[
            pl.BlockSpec((None, L, SSD_INNER), lambda b: (b, 0, COL_XS // SSD_INNER)),
            pl.BlockSpec((None, L, SSD_BC), lambda b: (b, 0, COL_BM // SSD_BC)),
            pl.BlockSpec((None, L, SSD_BC), lambda b: (b, 0, COL_CM // SSD_BC)),
            pl.BlockSpec((None, L, LANE), lambda b: (b, 0, COL_SMALL // LANE)),
            pl.BlockSpec((SSD_CONV, SSD_INNER), const2),
            pl.BlockSpec((SSD_CONV, SSD_BC), const2),
            pl.BlockSpec((SSD_CONV, SSD_BC), const2),
            pl.BlockSpec((1, SSD_INNER), const2),
            pl.BlockSpec((1, SSD_BC), const2),
            pl.BlockSpec((1, SSD_BC), const2),
            pl.BlockSpec((1, LANE), const2),
            pl.BlockSpec((1, LANE), const2),
            pl.BlockSpec((1, SSD_INNER), const2),
            pl.BlockSpec((2, SSD_CHUNK, SSD_CHUNK), const3),
            pl.BlockSpec((2, 2 * LANE, SSD_INNER), const3),
        ],
        out_specs=pl.BlockSpec((None, L, SSD_INNER), lambda b: (b, 0, 0)),
        out_shape=jax.ShapeDtypeStruct((B, L, SSD_INNER), BF16),
        scratch_shapes=[
            pltpu.VMEM((L + 2 * CONV_HALO, CONV_W), F32),
            pltpu.VMEM((L, SSD_INNER), F32),
            pltpu.VMEM((L, SSD_BC), BF16),
            pltpu.VMEM((L, SSD_BC), BF16),
            pltpu.VMEM((L, LANE), F32),
            pltpu.VMEM((L, SSD_INNER), F32),
            pltpu.VMEM((2, SSD_GROUPS, SSD_STATE, SSD_INNER // SSD_GROUPS), F32),
        ],
        compiler_params=_params("parallel"),
        name="ssd",
    )(u3, u3, u3, u3, cwx, cwb, cwc, cbx, cbb, cbc, dtb, aneg, dskip,
      jnp.asarray(tri, BF16), jnp.asarray(expm, BF16))


def _attend(q_blk, k, v):
    s = _dot(q_blk, k, NT)
    m = jnp.max(s, axis=-1, keepdims=True)
    p = jnp.exp(s - m)
    denom = jnp.sum(p, axis=-1, keepdims=True)
    return _dot(p.astype(BF16), v) * (1.0 / denom)


def _rope(x, cos, sin_a, sin_b, half):
    w = x.shape[-1]
    return x * cos + pltpu.roll(x, w - half, 1) * sin_a + pltpu.roll(x, half, 1) * sin_b


ATTN_TQ = 256
ATTN_TILES = 4


def _attn_tq(L):
    return ATTN_TQ if L % ATTN_TQ == 0 else L


def _attend_chain(load_q, load_k, load_v, finish):
    s = _dot(load_q(), load_k(), NT)
    yield
    m = jnp.max(s, axis=-1, keepdims=True)
    p = jnp.exp(s - m)
    r = 1.0 / jnp.sum(p, axis=-1, keepdims=True)
    pb = p.astype(BF16)
    yield
    finish(_dot(pb, load_v()) * r)


def _skewed(gens):
    pending = list(gens)
    active = []
    while pending or active:
        if pending:
            active.append(pending.pop(0))
        alive = []
        for g in active:
            try:
                next(g)
                alive.append(g)
            except StopIteration:
                pass
        active = alive


def _attn_loop(L, heads, load_q, load_k, load_v, store):
    tq = _attn_tq(L)
    n = L // tq
    tiles = ATTN_TILES if n % ATTN_TILES == 0 else 1

    def body(i, carry):
        chains = []
        for u in range(tiles):
            rs = pl.ds(pl.multiple_of((i * tiles + u) * tq, tq), tq)
            acc = []
            for h in range(heads):
                def finish(o, acc=acc, rs=rs, last=(h == heads - 1)):
                    acc.append(o)
                    if last:
                        store(rs, functools.reduce(lambda a, b: a + b, acc))
                chains.append(_attend_chain(functools.partial(load_q, h, rs), functools.partial(load_k, h),
                                            functools.partial(load_v, h), finish))
        _skewed(chains)
        return carry

    lax.fori_loop(0, n // tiles, body, 0)


MLA_PAIR = 2


def _mla_kernel(ql_ref, kv_ref, gq_ref, gkv_ref, wq_ref, wk_ref, wv_ref, cosq_ref, cosk_ref, sa_ref, sb_ref,
                o_ref, qn_s, kvn_s, kr_s, q_s, k_s, v_s):
    L = ql_ref.shape[0]
    tq = _attn_tq(L)
    half = MLA_ROPE // 4

    @pl.when(pl.program_id(1) == 0)
    def _():
        ql = ql_ref[:, 0:MLA_Q_RANK].astype(F32)
        ms = jnp.mean(ql * ql, axis=-1, keepdims=True)
        qn_s[...] = (ql * lax.rsqrt(ms + NORM_EPS) * gq_ref[...]).astype(BF16)
        kv = kv_ref[...].astype(F32)
        ms = jnp.mean(kv * kv, axis=-1, keepdims=True)
        kvn_s[...] = (kv * lax.rsqrt(ms + NORM_EPS) * gkv_ref[...]).astype(BF16)
        sm = ql_ref[:, MLA_Q_RANK:MLA_Q_RANK + LANE].astype(F32)
        kr_s[...] = _rope(sm, cosk_ref[...], sa_ref[...], sb_ref[...], half)

    scale = (MLA_NOPE + MLA_ROPE) ** -0.5
    for h in range(MLA_PAIR):
        q = _dot(qn_s[...], wq_ref[h])
        q = _rope(q, cosq_ref[...], sa_ref[...], sb_ref[...], half) * scale
        q_s[h] = q.astype(BF16)
        k_s[h] = (_dot(kvn_s[...], wk_ref[h]) + kr_s[...]).astype(BF16)
        v_s[h] = _dot(kvn_s[...], wv_ref[h]).astype(BF16)

    def store(rs, o):
        o_ref[rs, :] = o.astype(o_ref.dtype)

    _attn_loop(L, MLA_PAIR, lambda h, rs: q_s[h, rs, :], lambda h: k_s[h], lambda h: v_s[h], store)


def _rope_tables(L, d_rot, width, lane0):
    rows = L // GRID_W
    row = jnp.repeat(jnp.arange(rows), GRID_W).astype(F32)
    col = jnp.tile(jnp.arange(GRID_W), rows).astype(F32)
    m = d_rot // 2
    inv = ROPE_THETA ** (-jnp.arange(0, m, 2, dtype=F32) / m)
    ang_r = row[:, None] * inv
    ang_c = col[:, None] * inv
    ang = jnp.concatenate([ang_r, ang_r, ang_c, ang_c], axis=-1)
    cos, sin = jnp.cos(ang), jnp.sin(ang)
    first = (jnp.arange(d_rot) % m) < (m // 2)
    sin_a = jnp.where(first, -sin, 0.0)
    sin_b = jnp.where(first, 0.0, sin)
    if lane0 is None:
        reps = width // d_rot
        return tuple(jnp.tile(t, (1, reps)) for t in (cos, sin_a, sin_b))
    padl, padr = lane0, width - lane0 - d_rot
    return tuple(jnp.pad(t, ((0, 0), (padl, padr))) for t in (cos, sin_a, sin_b))


def _mla(u3, gq, gkv, w_q_b, w_kv_b):
    B, L, _ = u3.shape
    npair = MLA_HEADS // MLA_PAIR
    dq = MLA_NOPE + MLA_ROPE
    wq = w_q_b.reshape(MLA_Q_RANK, MLA_HEADS, dq).transpose(1, 0, 2)
    wq = jnp.pad(wq, ((0, 0), (0, 0), (0, LANE - dq))).astype(BF16)
    wkv = w_kv_b.reshape(MLA_KV_RANK, MLA_HEADS, MLA_NOPE + MLA_V).transpose(1, 0, 2)
    wk = jnp.pad(wkv[..., :MLA_NOPE], ((0, 0), (0, 0), (0, LANE - MLA_NOPE))).astype(BF16)
    wv_h = wkv[..., MLA_NOPE:]
    slot = (jnp.arange(MLA_HEADS) % MLA_PAIR)[:, None, None, None]
    wv = jnp.where(slot == jnp.arange(MLA_PAIR)[None, None, :, None], wv_h[:, :, None, :], 0.0)
    wv = wv.reshape(MLA_HEADS, MLA_KV_RANK, MLA_PAIR * MLA_V).astype(BF16)
    cos, sin_a, sin_b = _rope_tables(L, MLA_ROPE, LANE, SMALL_KROPE)
    cosq = cos.at[:, :MLA_NOPE].set(1.0)
    const2 = lambda b, p: (0, 0)
    return pl.pallas_call(
        _mla_kernel,
        grid=(B, npair),
        in_specs=[
            pl.BlockSpec((None, L, 512), lambda b, p: (b, 0, COL_QLAT // 512)),
            pl.BlockSpec((None, L, MLA_KV_RANK), lambda b, p: (b, 0, COL_KVLAT // MLA_KV_RANK)),
            pl.BlockSpec((1, MLA_Q_RANK), const2),
            pl.BlockSpec((1, MLA_KV_RANK), const2),
            pl.BlockSpec((MLA_PAIR, MLA_Q_RANK, LANE), lambda b, p: (p, 0, 0)),
            pl.BlockSpec((MLA_PAIR, MLA_KV_RANK, LANE), lambda b, p: (p, 0, 0)),
            pl.BlockSpec((MLA_PAIR, MLA_KV_RANK, LANE), lambda b, p: (p, 0, 0)),
            pl.BlockSpec((L, LANE), const2),
            pl.BlockSpec((L, LANE), const2),
            pl.BlockSpec((L, LANE), const2),
            pl.BlockSpec((L, LANE), const2),
        ],
        out_specs=pl.BlockSpec((None, L, LANE), lambda b, p: (b, 0, p)),
        out_shape=jax.ShapeDtypeStruct((B, L, MLA_WIDTH), BF16),
        scratch_shapes=[
            pltpu.VMEM((L, MLA_Q_RANK), BF16),
            pltpu.VMEM((L, MLA_KV_RANK), BF16),
            pltpu.VMEM((L, LANE), F32),
            pltpu.VMEM((MLA_PAIR, L, LANE), BF16),
            pltpu.VMEM((MLA_PAIR, L, LANE), BF16),
            pltpu.VMEM((MLA_PAIR, L, LANE), BF16),
        ],
        compiler_params=_params("parallel", "arbitrary"),
        name="mla",
    )(u3, u3, gq[None, :], gkv[None, :], wq, wk, wv, cosq, cos, sin_a, sin_b)


def _gqa_kernel(q_ref, kv_ref, gq_ref, gk_ref, bd_ref, rep_ref, cos_ref, sa_ref, sb_ref, o_ref,
                q_s, k_s, v_s):
    L = q_ref.shape[0]
    tq = _attn_tq(L)
    half = GQA_HEAD_DIM // 4
    kw = GQA_KV_HEADS * GQA_HEAD_DIM

    def headnorm(x, bd, gain):
        ms = _dot2(x * x, bd) * (1.0 / GQA_HEAD_DIM)
        return x * lax.rsqrt(ms + NORM_EPS) * gain

    q = headnorm(q_ref[...].astype(F32), bd_ref[...], gq_ref[...])
    q = _rope(q, cos_ref[...], sa_ref[...], sb_ref[...], half) * (GQA_HEAD_DIM ** -0.5)
    q_s[...] = q.astype(BF16)
    k = headnorm(kv_ref[:, 0:kw].astype(F32), bd_ref[0:kw, 0:kw], gk_ref[...])
    k = _rope(k, cos_ref[:, 0:kw], sa_ref[:, 0:kw], sb_ref[:, 0:kw], half)
    k_s[...] = _dot(k.astype(BF16), rep_ref[...]).astype(BF16)
    v_rep = _dot(kv_ref[:, kw:2 * kw], rep_ref[...]).astype(BF16)
    lane = lax.broadcasted_iota(jnp.int32, (L, GQA_GROUP_W), 1)
    for h in range(GQA_REP):
        v_s[h] = jnp.where(lane // GQA_HEAD_DIM == h, v_rep, jnp.zeros_like(v_rep))

    lane_q = lax.broadcasted_iota(jnp.int32, (tq, GQA_GROUP_W), 1)

    def load_q(h, rs):
        qb = q_s[rs, :]
        return jnp.where(lane_q // GQA_HEAD_DIM == h, qb, jnp.zeros_like(qb))

    def store(rs, o):
        o_ref[rs, :] = o.astype(o_ref.dtype)

    _attn_loop(L, GQA_REP, load_q, lambda h: k_s[...], lambda h: v_s[h], store)


def _gqa(u3, q_norm_g, k_norm_g):
    B, L, _ = u3.shape
    kw = GQA_KV_HEADS * GQA_HEAD_DIM
    gq = jnp.tile(q_norm_g, GQA_REP)[None, :]
    gk = jnp.tile(k_norm_g, GQA_KV_HEADS)[None, :]
    idx = np.arange(GQA_GROUP_W)
    bd = (idx[:, None] // GQA_HEAD_DIM == idx[None, :] // GQA_HEAD_DIM).astype(np.float32)
    rep = np.zeros((GQA_KV_HEADS, kw, GQA_GROUP_W), np.float32)
    for g in range(GQA_KV_HEADS):
        rep[g, g * GQA_HEAD_DIM + idx % GQA_HEAD_DIM, idx] = 1.0
    cos, sin_a, sin_b = _rope_tables(L, GQA_HEAD_DIM, GQA_GROUP_W, None)
    const2 = lambda b, g: (0, 0)
    return pl.pallas_call(
        _gqa_kernel,
        grid=(B, GQA_KV_HEADS),
        in_specs=[
            pl.BlockSpec((None, L, GQA_GROUP_W), lambda b, g: (b, 0, COL_QD // GQA_GROUP_W + g)),
            pl.BlockSpec((None, L, 2 * kw), lambda b, g: (b, 0, COL_KD // (2 * kw))),
            pl.BlockSpec((1, GQA_GROUP_W), const2),
            pl.BlockSpec((1, kw), const2),
            pl.BlockSpec((GQA_GROUP_W, GQA_GROUP_W), const2),
            pl.BlockSpec((None, kw, GQA_GROUP_W), lambda b, g: (g, 0, 0)),
            pl.BlockSpec((L, GQA_GROUP_W), const2),
            pl.BlockSpec((L, GQA_GROUP_W), const2),
            pl.BlockSpec((L, GQA_GROUP_W), const2),
        ],
        out_specs=pl.BlockSpec((None, L, GQA_GROUP_W), lambda b, g: (b, 0, g)),
        out_shape=jax.ShapeDtypeStruct((B, L, GQA_WIDTH), BF16),
        scratch_shapes=[
            pltpu.VMEM((L, GQA_GROUP_W), BF16),
            pltpu.VMEM((L, GQA_GROUP_W), BF16),
            pltpu.VMEM((GQA_REP, L, GQA_GROUP_W), BF16),
        ],
        compiler_params=_params("parallel", "parallel"),
        name="gqa",
    )(u3, u3, gq, gk, jnp.asarray(bd, BF16), jnp.asarray(rep, BF16), cos, sin_a, sin_b)


GLA_BLK = 2 * GLA_CHUNK


def _gla_kernel(q_ref, k_ref, v_ref, sm_ref, wg_ref, bg_ref, tri_ref, o_ref,
                glog_s, oacc_s, vt_s, st_s):
    L = q_ref.shape[0]
    Q = GLA_CHUNK
    nb = L // GLA_BLK

    def transpose_block(b, carry):
        rows = pl.ds(pl.multiple_of(b * GLA_BLK, GLA_BLK), GLA_BLK)
        vt_s[b] = v_ref[rows, :].astype(F32).T.astype(BF16)
        return carry

    lax.fori_loop(0, nb, transpose_block, 0, unroll=2 if nb % 2 == 0 else 1)

    sm = sm_ref[...]
    g_pre = _dot(jnp.concatenate([sm, sm], axis=1), wg_ref[...]) + bg_ref[...]
    glog_s[...] = (jnp.minimum(g_pre, 0.0) - jnp.log(1.0 + jnp.exp(-jnp.abs(g_pre)))) * (1.0 / GLA_TAU)

    li = lax.broadcasted_iota(jnp.int32, (Q, Q), 0)
    si = lax.broadcasted_iota(jnp.int32, (Q, Q), 1)
    lane = lax.broadcasted_iota(jnp.int32, (Q, GLA_QK), 1)
    lane_s = lax.broadcasted_iota(jnp.int32, (GLA_DV, GLA_QK), 1)
    k_zero = jnp.zeros((Q, GLA_QK), BF16)

    def chunk_step(blk, j, d, first):
        rs = pl.ds(pl.multiple_of(blk * GLA_BLK + j * Q, Q), Q)
        keep = (si <= li) if d == 0 else (si >= li)
        g_c = glog_s[rs, d * GLA_QK:(d + 1) * GLA_QK]
        g_hi, g_lo = _split(g_c)
        gcum = _dot(tri_ref[d], g_hi) + _dot(tri_ref[d], g_lo)
        yield
        g_last = gcum[Q - 1:Q, :] if d == 0 else gcum[0:1, :]
        q = q_ref[rs, :].astype(F32) * (GLA_DK ** -0.5)
        k = k_ref[rs, :].astype(F32)
        v = v_ref[rs, :]
        qg = (q * jnp.exp(gcum)).astype(BF16)
        kg = (k * jnp.exp(-gcum)).astype(BF16)
        k_end = (k * jnp.exp(g_last - gcum)).astype(BF16)
        k_pad = jnp.concatenate([k_end, k_zero] if j == 0 else [k_zero, k_end], axis=0)
        u_t = _dot(vt_s[blk], k_pad)
        qhs = [jnp.where(lane // GLA_DK == h, qg, jnp.zeros_like(qg)) for h in range(GLA_HEADS)]
        yield
        atts = [jnp.where(keep, _dot(qh, kg, NT), 0.0).astype(BF16) for qh in qhs]
        yield
        vss = [slice(h * GLA_DV, (h + 1) * GLA_DV) for h in range(GLA_HEADS)]
        o_intra = [_dot(att, v[:, vs]) for att, vs in zip(atts, vss)]
        yield
        s_prev = st_s[d]
        s_prev_b = s_prev.astype(BF16)
        s_new = jnp.exp(g_last) * s_prev
        for h, vs in enumerate(vss):
            o_h = o_intra[h] + _dot(qhs[h], s_prev_b, NT)
            if first:
                oacc_s[rs, vs] = o_h
            else:
                o_ref[rs, vs] = (oacc_s[rs, vs] + o_h).astype(o_ref.dtype)
            s_new = s_new + jnp.where(lane_s // GLA_DK == h, u_t[vs, :], 0.0)
        st_s[d] = s_new

    st_s[...] = jnp.zeros(st_s.shape, F32)
    assert nb % 2 == 0
    half = nb // 2

    def make(first):
        def body(i, carry):
            _interleave([chunk_step(i, 0, 0, first), chunk_step(nb - 1 - i, 1, 1, first),
                         chunk_step(i, 1, 0, first), chunk_step(nb - 1 - i, 0, 1, first)])
            return carry
        return body

    lax.fori_loop(0, half, make(True), 0)
    lax.fori_loop(half, nb, make(False), 0)


def _gla(u3, w_gate_up, b_gate):
    B, L, _ = u3.shape
    wg = jnp.zeros((LANE, 2 * GLA_QK), F32)
    for d in range(2):
        r0 = SMALL_GLR + d * GLA_GATE_RANK
        wg = wg.at[r0:r0 + GLA_GATE_RANK, d * GLA_QK:(d + 1) * GLA_QK].set(w_gate_up[d])
    wg_hi, wg_lo = _split(wg)
    wg2 = jnp.concatenate([wg_hi, wg_lo], axis=0)
    bg = b_gate.reshape(1, 2 * GLA_QK)
    l_idx = np.arange(GLA_CHUNK)
    tri = np.stack([l_idx[None, :] <= l_idx[:, None], l_idx[None, :] >= l_idx[:, None]]).astype(np.float32)
    const2 = lambda b: (0, 0)
    return pl.pallas_call(
        _gla_kernel,
        grid=(B,),
        in_specs=[
            pl.BlockSpec((None, L, GLA_QK), lambda b: (b, 0, COL_QC // GLA_QK)),
            pl.BlockSpec((None, L, GLA_QK), lambda b: (b, 0, COL_KC // GLA_QK)),
            pl.BlockSpec((None, L, GLA_WIDTH), lambda b: (b, 0, COL_VC // GLA_WIDTH)),
            pl.BlockSpec((None, L, LANE), lambda b: (b, 0, COL_SMALL // LANE)),
            pl.BlockSpec((2 * LANE, 2 * GLA_QK), const2),
            pl.BlockSpec((1, 2 * GLA_QK), const2),
            pl.BlockSpec((2, GLA_CHUNK, GLA_CHUNK), lambda b: (0, 0, 0)),
        ],
        out_specs=pl.BlockSpec((None, L, GLA_WIDTH), lambda b: (b, 0, 0)),
        out_shape=jax.ShapeDtypeStruct((B, L, GLA_WIDTH), BF16),
        scratch_shapes=[
            pltpu.VMEM((L, 2 * GLA_QK), F32),
            pltpu.VMEM((L, GLA_WIDTH), F32),
            pltpu.VMEM((L // GLA_BLK, GLA_WIDTH, GLA_BLK), BF16),
            pltpu.VMEM((2, GLA_DV, GLA_QK), F32),
        ],
        compiler_params=_params("parallel"),
        name="gla",
    )(u3, u3, u3, u3, wg2, bg, jnp.asarray(tri, BF16))


def _merge_kernel(gm_ref, za_ref, zb_ref, zc_ref, zd_ref, ya_ref, ob_ref, oc_ref, od_ref, x_ref,
                  wa_ref, wb_ref, wc_ref, wd_ref, wo_ref, ga_ref, gc_ref, fg_ref, o_ref, *, final_norm):
    def silu_b(z):
        hz = z * 0.5
        return hz + hz * jnp.tanh(hz)

    def gated(o_ref_, z_ref_):
        return o_ref_[...] * silu_b(z_ref_[...])

    ya = gated(ya_ref, za_ref).astype(F32)
    ms = jnp.mean(ya * ya, axis=-1, keepdims=True)
    ya = (ya * lax.rsqrt(ms + NORM_EPS) * ga_ref[...]).astype(BF16)
    mixed = _sigmoid(gm_ref[:, 0:D_MODEL].astype(F32)) * _dot(ya, wa_ref[...])

    yb = gated(ob_ref, zb_ref)
    mixed = mixed + _sigmoid(gm_ref[:, D_MODEL:2 * D_MODEL].astype(F32)) * _dot(yb, wb_ref[...])

    oc = oc_ref[...].astype(F32)
    parts = []
    for h in range(GLA_HEADS):
        oh = oc[:, h * GLA_DV:(h + 1) * GLA_DV]
        ms = jnp.mean(oh * oh, axis=-1, keepdims=True)
        parts.append(oh * lax.rsqrt(ms + NORM_EPS))
    yc = (jnp.concatenate(parts, axis=1) * gc_ref[...]).astype(BF16) * silu_b(zc_ref[...])
    mixed = mixed + _sigmoid(gm_ref[:, 2 * D_MODEL:3 * D_MODEL].astype(F32)) * _dot(yc, wc_ref[...])

    yd = gated(od_ref, zd_ref)
    mixed = mixed + _sigmoid(gm_ref[:, 3 * D_MODEL:4 * D_MODEL].astype(F32)) * _dot(yd, wd_ref[...])

    x = x_ref[...] + _dot(mixed.astype(BF16), wo_ref[...])
    if final_norm:
        ms = jnp.mean(x * x, axis=-1, keepdims=True)
        x = x * lax.rsqrt(ms + NORM_EPS) * fg_ref[...]
    o_ref[...] = x


def _merge(u2, ya, ob, oc, od, x2, wa, wb, wc, wd, wo, ga, gc, fg, final_norm):
    t = x2.shape[0]
    tm = min(256, t)
    row = lambda c: (lambda i: (i, c))
    const2 = lambda i: (0, 0)
    return pl.pallas_call(
        functools.partial(_merge_kernel, final_norm=final_norm),
        grid=(t // tm,),
        in_specs=[
            pl.BlockSpec((tm, N_BRANCH * D_MODEL), row(COL_GMERGE // (N_BRANCH * D_MODEL))),
            pl.BlockSpec((tm, SSD_INNER), row(COL_ZA // SSD_INNER)),
            pl.BlockSpec((tm, MLA_WIDTH), row(COL_ZB // MLA_WIDTH)),
            pl.BlockSpec((tm, GLA_WIDTH), row(COL_ZC // GLA_WIDTH)),
            pl.BlockSpec((tm, GQA_WIDTH), row(COL_ZD // GQA_WIDTH)),
            pl.BlockSpec((tm, SSD_INNER), row(0)),
            pl.BlockSpec((tm, MLA_WIDTH), row(0)),
            pl.BlockSpec((tm, GLA_WIDTH), row(0)),
            pl.BlockSpec((tm, GQA_WIDTH), row(0)),
            pl.BlockSpec((tm, D_MODEL), row(0)),
            pl.BlockSpec((SSD_INNER, D_MODEL), const2),
            pl.BlockSpec((MLA_WIDTH, D_MODEL), const2),
            pl.BlockSpec((GLA_WIDTH, D_MODEL), const2),
            pl.BlockSpec((GQA_WIDTH, D_MODEL), const2),
            pl.BlockSpec((D_MODEL, D_MODEL), const2),
            pl.BlockSpec((1, SSD_INNER), const2),
            pl.BlockSpec((1, GLA_WIDTH), const2),
            pl.BlockSpec((1, D_MODEL), const2),
        ],
        out_specs=pl.BlockSpec((tm, D_MODEL), row(0)),
        out_shape=jax.ShapeDtypeStruct((t, D_MODEL), F32),
        compiler_params=_params("parallel"),
        name="merge",
    )(u2, u2, u2, u2, u2, ya, ob, oc, od, x2, wa, wb, wc, wd, wo, ga, gc, fg)


def _reorder_w_in(w):
    split_idx = [int(v) for v in np.cumsum(IN_WIDTHS)[:-1]]
    (g_merge, z_a, xbc, dt_raw, z_b, q_lat, kv_lat, k_rope, z_c, q_c, k_c, v_c, g_lr,
     z_d, q_d, k_d, v_d) = jnp.split(w, split_idx, axis=1)
    xs, bm, cm = xbc[:, :SSD_INNER], xbc[:, SSD_INNER:SSD_INNER + SSD_BC], xbc[:, SSD_INNER + SSD_BC:]
    small = jnp.concatenate([dt_raw, g_lr, k_rope, jnp.zeros((w.shape[0], LANE - 96), w.dtype)], axis=1)
    cols = [g_merge, z_a, xs, z_b, z_c, z_d, q_d, v_c, q_lat, small, bm, cm, kv_lat, q_c, k_c, k_d, v_d]
    out = jnp.concatenate(cols, axis=1)
    assert out.shape[1] == N_PROJ
    return out.astype(BF16)


def kernel(x, norm_g, w_in, conv_w, conv_b, a_log, dt_bias, d_skip, ssd_norm_g, q_lat_norm_g, kv_lat_norm_g, w_q_b, w_kv_b, w_gate_up, b_gate, gla_norm_g, q_norm_g, k_norm_g, w_br_a, w_br_b, w_br_c, w_br_d, w_out, final_g):
    B, L, D = x.shape
    depth = w_in.shape[0]
    x2 = x.reshape(B * L, D)
    for i in range(depth):
        u2 = _inproj(x2, norm_g[i][None, :], _reorder_w_in(w_in[i]))
        u3 = u2.reshape(B, L, N_PROJ)
        ya = _ssd(u3, conv_w[i], conv_b[i], a_log[i], dt_bias[i], d_skip[i])
        ob = _mla(u3, q_lat_norm_g[i], kv_lat_norm_g[i], w_q_b[i], w_kv_b[i])
        oc = _gla(u3, w_gate_up[i], b_gate[i])
        od = _gqa(u3, q_norm_g[i], k_norm_g[i])
        x2 = _merge(
            u2, ya.reshape(B * L, -1), ob.reshape(B * L, -1), oc.reshape(B * L, -1), od.reshape(B * L, -1), x2,
            w_br_a[i].astype(BF16), w_br_b[i].astype(BF16), w_br_c[i].astype(BF16), w_br_d[i].astype(BF16),
            w_out[i].astype(BF16), ssd_norm_g[i][None, :], gla_norm_g[i][None, :], final_g[None, :],
            final_norm=(i == depth - 1))
    return x2.reshape(B, L, D)
```

```python
import functools

import numpy as np
import jax
import jax.numpy as jnp
from jax import lax
from jax.experimental import pallas as pl
from jax.experimental.pallas import tpu as pltpu

F32 = jnp.float32
BF16 = jnp.bfloat16

D_MODEL = 1024
GRID_W = 64
ROPE_THETA = 10000.0
NORM_EPS = 1e-6
N_BRANCH = 4

SSD_HEADS = 16
SSD_HEAD_DIM = 64
SSD_INNER = SSD_HEADS * SSD_HEAD_DIM
SSD_GROUPS = 2
SSD_STATE = 128
SSD_CONV = 5
SSD_CHUNK = 128
SSD_BC = SSD_GROUPS * SSD_STATE

MLA_HEADS = 8
MLA_Q_RANK = 384
MLA_KV_RANK = 256
MLA_NOPE = 64
MLA_ROPE = 32
MLA_V = 64
MLA_WIDTH = MLA_HEADS * MLA_V

GLA_HEADS = 4
GLA_DK = 64
GLA_DV = 128
GLA_GATE_RANK = 16
GLA_TAU = 16.0
GLA_CHUNK = 64
GLA_WIDTH = GLA_HEADS * GLA_DV
GLA_QK = GLA_HEADS * GLA_DK

GQA_HEADS = 8
GQA_KV_HEADS = 2
GQA_HEAD_DIM = 64
GQA_WIDTH = GQA_HEADS * GQA_HEAD_DIM
GQA_REP = GQA_HEADS // GQA_KV_HEADS
GQA_GROUP_W = GQA_REP * GQA_HEAD_DIM

IN_WIDTHS = (
    N_BRANCH * D_MODEL, SSD_INNER, SSD_INNER + 2 * SSD_BC, 2 * SSD_HEADS, MLA_WIDTH, MLA_Q_RANK,
    MLA_KV_RANK, MLA_ROPE, GLA_WIDTH, GLA_QK, GLA_QK, GLA_WIDTH, 2 * GLA_GATE_RANK, GQA_WIDTH,
    GQA_WIDTH, GQA_KV_HEADS * GQA_HEAD_DIM, GQA_KV_HEADS * GQA_HEAD_DIM,
)

LANE = 128

SMALL_DT = 0
SMALL_GLR = 32
SMALL_KROPE = 64
COL_GMERGE = 0
COL_ZA = 4096
COL_XS = 5120
COL_ZB = 6144
COL_ZC = 6656
COL_ZD = 7168
COL_QD = 7680
COL_VC = 8192
COL_QLAT = 8704
COL_SMALL = COL_QLAT + MLA_Q_RANK
COL_BM = 9216
COL_CM = 9472
COL_KVLAT = 9728
COL_QC = 9984
COL_KC = 10240
COL_KD = 10496
N_PROJ = 10752

VMEM_LIMIT = 56 * 1024 * 1024

NN = ((1,), (0,))
NT = ((1,), (1,))
TN = ((0,), (0,))


def _dot(a, b, dims=NN):
    return lax.dot_general(a, b, (dims, ((), ())), preferred_element_type=F32)


def _split(a):
    hi = a.astype(BF16)
    lo = (a - hi.astype(F32)).astype(BF16)
    return hi, lo


def _dot2(a, m, dims=NN):
    hi, lo = _split(a)
    return _dot(hi, m, dims) + _dot(lo, m, dims)


def _softplus(x):
    return jnp.maximum(x, 0.0) + jnp.log1p(jnp.exp(-jnp.abs(x)))


def _silu(x):
    return x * (1.0 / (1.0 + jnp.exp(-x)))


def _params(*sem):
    return pltpu.CompilerParams(dimension_semantics=sem, vmem_limit_bytes=VMEM_LIMIT)


def _interleave(gens):
    gens = list(gens)
    while gens:
        alive = []
        for g in gens:
            try:
                next(g)
                alive.append(g)
            except StopIteration:
                pass
        gens = alive


def _both_directions(nc, chunk_step, unroll=1):
    assert nc % 2 == 0
    half = nc // 2
    if half % unroll:
        unroll = 1

    def make(first, base):
        def body(i, carry):
            chains = []
            for u in range(unroll):
                c = base + i * unroll + u
                chains += [chunk_step(c, 0, first), chunk_step(nc - 1 - c, 1, first)]
            _interleave(chains)
            return carry
        return body

    lax.fori_loop(0, half // unroll, make(True, 0), 0)
    lax.fori_loop(0, half // unroll, make(False, half), 0)


def _inproj_kernel(x_ref, g_ref, w_ref, o_ref, h_ref):
    @pl.when(pl.program_id(1) == 0)
    def _():
        x = x_ref[...]
        ms = jnp.mean(x * x, axis=-1, keepdims=True)
        h_ref[...] = (x * lax.rsqrt(ms + NORM_EPS) * g_ref[...]).astype(BF16)

    o_ref[...] = _dot(h_ref[...], w_ref[...]).astype(BF16)


def _inproj(x2, g, w):
    t = x2.shape[0]
    tm = min(1024, t)
    tn = 1536
    return pl.pallas_call(
        _inproj_kernel,
        grid=(t // tm, N_PROJ // tn),
        in_specs=[
            pl.BlockSpec((tm, D_MODEL), lambda i, j: (i, 0)),
            pl.BlockSpec((1, D_MODEL), lambda i, j: (0, 0)),
            pl.BlockSpec((D_MODEL, tn), lambda i, j: (0, j)),
        ],
        out_specs=pl.BlockSpec((tm, tn), lambda i, j: (i, j)),
        out_shape=jax.ShapeDtypeStruct((t, N_PROJ), BF16),
        scratch_shapes=[pltpu.VMEM((tm, D_MODEL), BF16)],
        compiler_params=_params("parallel", "arbitrary"),
        name="inproj",
    )(x2, g, w)


CONV_W = 256
CONV_HALO = 8


def _ssd_kernel(xs_ref, bm_ref, cm_ref, sm_ref, cwx_ref, cwb_ref, cwc_ref, cbx_ref, cbb_ref, cbc_ref,
                dtb_ref, aneg_ref, dskip_ref, tri_ref, exp_ref, o_ref,
                pad_s, xs_s, bm_s, cm_s, dt_s, yacc_s, st_s):
    L = xs_ref.shape[0]
    Q = SSD_CHUNK
    nc = L // Q
    ext = Q + 2 * CONV_HALO

    pad_s[0:CONV_HALO, :] = jnp.zeros((CONV_HALO, CONV_W), F32)
    pad_s[L + CONV_HALO:L + 2 * CONV_HALO, :] = jnp.zeros((CONV_HALO, CONV_W), F32)

    def conv_group(src_ref, w_ref, b_ref, col, dst_ref, dst_dtype):
        pad_s[CONV_HALO:L + CONV_HALO, :] = src_ref[:, col:col + CONV_W].astype(F32)
        w = w_ref[:, col:col + CONV_W]
        b = b_ref[:, col:col + CONV_W]

        def body(r, carry):
            base = pl.multiple_of(r * Q, Q)
            win = pad_s[pl.ds(base, ext), :]
            acc = jnp.zeros((Q, CONV_W), F32) + b
            for k in range(SSD_CONV):
                shift = (SSD_CONV // 2 - k) % ext
                sh = win if shift == 0 else pltpu.roll(win, shift, 0)
                acc = acc + sh[CONV_HALO:CONV_HALO + Q, :] * w[k:k + 1, :]
            dst_ref[pl.ds(base, Q), col:col + CONV_W] = _silu(acc).astype(dst_dtype)
            return carry

        lax.fori_loop(0, nc, body, 0)

    for cg in range(SSD_INNER // CONV_W):
        conv_group(xs_ref, cwx_ref, cbx_ref, cg * CONV_W, xs_s, F32)
    conv_group(bm_ref, cwb_ref, cbb_ref, 0, bm_s, BF16)
    conv_group(cm_ref, cwc_ref, cbc_ref, 0, cm_s, BF16)

    dt_s[...] = _softplus(sm_ref[...].astype(F32) + dtb_ref[...])

    li = lax.broadcasted_iota(jnp.int32, (Q, Q), 0)
    si = lax.broadcasted_iota(jnp.int32, (Q, Q), 1)
    lane = lax.broadcasted_iota(jnp.int32, (Q, SSD_INNER), 1)
    low_half = (lane % LANE) < SSD_HEAD_DIM
    gw = SSD_INNER // SSD_GROUPS
    heads_per_group = SSD_HEADS // SSD_GROUPS

    def chunk_step(c, d, first):
        rs = pl.ds(pl.multiple_of(c * Q, Q), Q)
        keep = (si <= li) if d == 0 else (si >= li)
        tri = tri_ref[d]
        dt_c = dt_s[rs, :]
        da_c = dt_c * aneg_ref[...]
        da_hi, da_lo = _split(da_c)
        ac = _dot(tri, da_hi) + _dot(tri, da_lo)
        tri_t = tri_ref[1 - d]
        ac_t = _dot(da_hi, tri_t, TN) + _dot(da_lo, tri_t, TN)
        yield
        ac_hi, ac_lo = _split(ac)
        ac_x = _dot(jnp.concatenate([ac_hi, ac_lo], axis=1), exp_ref[d])
        dt_hi, dt_lo = _split(dt_c)
        dt_x = _dot(jnp.concatenate([dt_hi, dt_lo], axis=1), exp_ref[d])
        cm_c = cm_s[rs, :]
        bm_c = bm_s[rs, :]
        cmgs = [cm_c[:, g * SSD_STATE:(g + 1) * SSD_STATE] for g in range(SSD_GROUPS)]
        bmgs = [bm_c[:, g * SSD_STATE:(g + 1) * SSD_STATE] for g in range(SSD_GROUPS)]
        cbs = [_dot(cmg, bmg, NT) for cmg, bmg in zip(cmgs, bmgs)]
        yield
        xs_c = xs_s[rs, :]
        xd = xs_c * dt_x
        xd_b = xd.astype(BF16)
        a_last = ac_x[Q - 1:Q, :] if d == 0 else ac_x[0:1, :]
        xdd = (xd * jnp.exp(a_last - ac_x)).astype(BF16)
        eac = jnp.exp(ac_x)
        xd_lo = jnp.where(low_half, xd_b, jnp.zeros_like(xd_b))
        xd_hi = jnp.where(low_half, jnp.zeros_like(xd_b), xd_b)
        st_news = [_dot(bmgs[g], xdd[:, g * gw:(g + 1) * gw], TN) for g in range(SSD_GROUPS)]
        yield
        y_diag = {}
        for g in range(SSD_GROUPS):
            for pair in range(heads_per_group // 2):
                c0 = g * gw + pair * LANE
                lms = []
                for q in range(2):
                    e = d * SSD_HEADS + g * heads_per_group + 2 * pair + q
                    seg = ac[:, e:e + 1] - ac_t[e:e + 1, :]
                    lms.append((jnp.where(keep, jnp.exp(seg), 0.0) * cbs[g]).astype(BF16))
                x_pair = jnp.concatenate([xd_lo[:, c0:c0 + LANE], xd_hi[:, c0:c0 + LANE]], axis=0)
                y = _dot(jnp.concatenate(lms, axis=1), x_pair)
                if d == 0:
                    y = y + xs_c[:, c0:c0 + LANE] * dskip_ref[:, c0:c0 + LANE]
                y_diag[c0] = y
            yield
        for g in range(SSD_GROUPS):
            s_prev = st_s[d, g]
            y_off = _dot(cmgs[g], s_prev.astype(BF16)) * eac[:, g * gw:(g + 1) * gw]
            st_s[d, g] = jnp.exp(a_last[:, g * gw:(g + 1) * gw]) * s_prev + st_news[g]
            for pair in range(heads_per_group // 2):
                c0 = g * gw + pair * LANE
                y_pair = y_diag[c0] + y_off[:, pair * LANE:(pair + 1) * LANE]
                if first:
                    yacc_s[rs, c0:c0 + LANE] = y_pair
                else:
                    o_ref[rs, c0:c0 + LANE] = (yacc_s[rs, c0:c0 + LANE] + y_pair).astype(o_ref.dtype)

    st_s[...] = jnp.zeros(st_s.shape, F32)
    _both_directions(nc, chunk_step, unroll=2)


def _ssd(u3, conv_w, conv_b, a_log, dt_bias, d_skip):
    B, L, _ = u3.shape
    cwx, cwb, cwc = conv_w[:, :SSD_INNER], conv_w[:, SSD_INNER:SSD_INNER + SSD_BC], conv_w[:, SSD_INNER + SSD_BC:]
    cb2 = conv_b[None, :]
    cbx, cbb, cbc = cb2[:, :SSD_INNER], cb2[:, SSD_INNER:SSD_INNER + SSD_BC], cb2[:, SSD_INNER + SSD_BC:]
    zpad = jnp.zeros((LANE - 2 * SSD_HEADS,), F32)
    dtb = jnp.concatenate([dt_bias.reshape(-1), zpad])[None, :]
    aneg = jnp.concatenate([-jnp.exp(a_log.reshape(-1)), zpad])[None, :]
    dskip = jnp.repeat(d_skip, SSD_HEAD_DIM)[None, :]
    l_idx = np.arange(SSD_CHUNK)
    tri = np.stack([l_idx[None, :] <= l_idx[:, None], l_idx[None, :] >= l_idx[:, None]]).astype(np.float32)
    expm = np.zeros((2, 2 * LANE, SSD_INNER), np.float32)
    for d in range(2):
        for e in range(SSD_HEADS):
            for part in range(2):
                expm[d, part * LANE + d * SSD_HEADS + e, e * SSD_HEAD_DIM:(e + 1) * SSD_HEAD_DIM] = 1.0
    const2 = lambda b: (0, 0)
    const3 = lambda b: (0, 0, 0)
    return pl.pallas_call(
        _ssd_kernel,
        grid=(B,),
        in_specs=[
            pl.BlockSpec((None, L, SSD_INNER), lambda b: (b, 0, COL_XS // SSD_INNER)),
            pl.BlockSpec((None, L, SSD_BC), lambda b: (b, 0, COL_BM // SSD_BC)),
            pl.BlockSpec((None, L, SSD_BC), lambda b: (b, 0, COL_CM // SSD_BC)),
            pl.BlockSpec((None, L, LANE), lambda b: (b, 0, COL_SMALL // LANE)),
            pl.BlockSpec((SSD_CONV, SSD_INNER), const2),
            pl.BlockSpec((SSD_CONV, SSD_BC), const2),
            pl.BlockSpec((SSD_CONV, SSD_BC), const2),
            pl.BlockSpec((1, SSD_INNER), const2),
            pl.BlockSpec((1, SSD_BC), const2),
            pl.BlockSpec((1, SSD_BC), const2),
            pl.BlockSpec((1, LANE), const2),
            pl.BlockSpec((1, LANE), const2),
            pl.BlockSpec((1, SSD_INNER), const2),
            pl.BlockSpec((2, SSD_CHUNK, SSD_CHUNK), const3),
            pl.BlockSpec((2, 2 * LANE, SSD_INNER), const3),
        ],
        out_specs=pl.BlockSpec((None, L, SSD_INNER), lambda b: (b, 0, 0)),
        out_shape=jax.ShapeDtypeStruct((B, L, SSD_INNER), BF16),
        scratch_shapes=[
            pltpu.VMEM((L + 2 * CONV_HALO, CONV_W), F32),
            pltpu.VMEM((L, SSD_INNER), F32),
            pltpu.VMEM((L, SSD_BC), BF16),
            pltpu.VMEM((L, SSD_BC), BF16),
            pltpu.VMEM((L, LANE), F32),
            pltpu.VMEM((L, SSD_INNER), F32),
            pltpu.VMEM((2, SSD_GROUPS, SSD_STATE, SSD_INNER // SSD_GROUPS), F32),
        ],
        compiler_params=_params("parallel"),
        name="ssd",
    )(u3, u3, u3, u3, cwx, cwb, cwc, cbx, cbb, cbc, dtb, aneg, dskip,
      jnp.asarray(tri, BF16), jnp.asarray(expm, BF16))


LOG2E = 1.4426950408889634


def _rope(x, cos, sin_a, sin_b, half):
    w = x.shape[-1]
    return x * cos + pltpu.roll(x, w - half, 1) * sin_a + pltpu.roll(x, half, 1) * sin_b


ATTN_TQ = 256
ATTN_CHAINS = 16


def _attn_tq(L):
    return ATTN_TQ if L % ATTN_TQ == 0 else L


def _attend_chain(load_q, load_k, load_v, finish):
    s = _dot(load_q(), load_k(), NT)
    yield
    m = jnp.max(s, axis=-1, keepdims=True)
    p = jnp.exp2(s - m)
    r = 1.0 / jnp.sum(p, axis=-1, keepdims=True)
    pb = p.astype(BF16)
    yield
    finish(_dot(pb, load_v()) * r)


def _skewed(gens):
    pending = list(gens)
    active = []
    while pending or active:
        if pending:
            active.append(pending.pop(0))
        alive = []
        for g in active:
            try:
                next(g)
                alive.append(g)
            except StopIteration:
                pass
        active = alive


def _attn_loop(L, heads, load_q, load_k, load_v, store):
    tq = _attn_tq(L)
    n = L // tq
    tiles = ATTN_CHAINS // heads
    if n % tiles:
        tiles = 1

    def body(i, carry):
        chains = []
        for u in range(tiles):
            rs = pl.ds(pl.multiple_of((i * tiles + u) * tq, tq), tq)
            acc = []
            for h in range(heads):
                def finish(o, acc=acc, rs=rs, last=(h == heads - 1)):
                    acc.append(o)
                    if last:
                        store(rs, functools.reduce(lambda a, b: a + b, acc))
                chains.append(_attend_chain(functools.partial(load_q, h, rs), functools.partial(load_k, h),
                                            functools.partial(load_v, h), finish))
        _skewed(chains)
        return carry

    lax.fori_loop(0, n // tiles, body, 0)


MLA_PAIR = 2


def _mla_kernel(ql_ref, kv_ref, gq_ref, gkv_ref, wq_ref, wk_ref, wv_ref, cosq_ref, cosk_ref, sa_ref, sb_ref,
                o_ref, qn_s, kvn_s, kr_s, q_s, k_s, v_s):
    L = ql_ref.shape[0]
    tq = _attn_tq(L)
    half = MLA_ROPE // 4

    @pl.when(pl.program_id(1) == 0)
    def _():
        ql = ql_ref[:, 0:MLA_Q_RANK].astype(F32)
        ms = jnp.mean(ql * ql, axis=-1, keepdims=True)
        qn_s[...] = (ql * lax.rsqrt(ms + NORM_EPS) * gq_ref[...]).astype(BF16)
        kv = kv_ref[...].astype(F32)
        ms = jnp.mean(kv * kv, axis=-1, keepdims=True)
        kvn_s[...] = (kv * lax.rsqrt(ms + NORM_EPS) * gkv_ref[...]).astype(BF16)
        sm = ql_ref[:, MLA_Q_RANK:MLA_Q_RANK + LANE].astype(F32)
        kr_s[...] = _rope(sm, cosk_ref[...], sa_ref[...], sb_ref[...], half)

    scale = (MLA_NOPE + MLA_ROPE) ** -0.5 * LOG2E
    q2 = _dot(qn_s[...], wq_ref[...])
    k2 = _dot(kvn_s[...], wk_ref[...])
    v2 = _dot(kvn_s[...], wv_ref[...]).astype(BF16)
    lane_v = lax.broadcasted_iota(jnp.int32, (L, LANE), 1)
    for h in range(MLA_PAIR):
        hs = slice(h * LANE, (h + 1) * LANE)
        q = _rope(q2[:, hs], cosq_ref[...], sa_ref[...], sb_ref[...], half) * scale
        q_s[h] = q.astype(BF16)
        k_s[h] = (k2[:, hs] + kr_s[...]).astype(BF16)
        v_s[h] = jnp.where(lane_v // MLA_V == h, v2, jnp.zeros_like(v2))

    def store(rs, o):
        o_ref[rs, :] = o.astype(o_ref.dtype)

    _attn_loop(L, MLA_PAIR, lambda h, rs: q_s[h, rs, :], lambda h: k_s[h], lambda h: v_s[h], store)


def _rope_tables(L, d_rot, width, lane0):
    rows = L // GRID_W
    row = jnp.repeat(jnp.arange(rows), GRID_W).astype(F32)
    col = jnp.tile(jnp.arange(GRID_W), rows).astype(F32)
    m = d_rot // 2
    inv = ROPE_THETA ** (-jnp.arange(0, m, 2, dtype=F32) / m)
    ang_r = row[:, None] * inv
    ang_c = col[:, None] * inv
    ang = jnp.concatenate([ang_r, ang_r, ang_c, ang_c], axis=-1)
    cos, sin = jnp.cos(ang), jnp.sin(ang)
    first = (jnp.arange(d_rot) % m) < (m // 2)
    sin_a = jnp.where(first, -sin, 0.0)
    sin_b = jnp.where(first, 0.0, sin)
    if lane0 is None:
        reps = width // d_rot
        return tuple(jnp.tile(t, (1, reps)) for t in (cos, sin_a, sin_b))
    padl, padr = lane0, width - lane0 - d_rot
    return tuple(jnp.pad(t, ((0, 0), (padl, padr))) for t in (cos, sin_a, sin_b))


def _mla(u3, gq, gkv, w_q_b, w_kv_b):
    B, L, _ = u3.shape
    npair = MLA_HEADS // MLA_PAIR
    dq = MLA_NOPE + MLA_ROPE
    wq = w_q_b.reshape(MLA_Q_RANK, MLA_HEADS, dq).transpose(1, 0, 2)
    wq = jnp.pad(wq, ((0, 0), (0, 0), (0, LANE - dq))).astype(BF16)
    wkv = w_kv_b.reshape(MLA_KV_RANK, MLA_HEADS, MLA_NOPE + MLA_V).transpose(1, 0, 2)
    wk = jnp.pad(wkv[..., :MLA_NOPE], ((0, 0), (0, 0), (0, LANE - MLA_NOPE))).astype(BF16)
    wv = wkv[..., MLA_NOPE:].astype(BF16)

    def by_pair(w):
        h, k, n = w.shape
        return w.reshape(npair, MLA_PAIR, k, n).transpose(0, 2, 1, 3).reshape(npair, k, MLA_PAIR * n)

    wq, wk, wv = by_pair(wq), by_pair(wk), by_pair(wv)
    cos, sin_a, sin_b = _rope_tables(L, MLA_ROPE, LANE, SMALL_KROPE)
    cosq = cos.at[:, :MLA_NOPE].set(1.0)
    const2 = lambda b, p: (0, 0)
    return pl.pallas_call(
        _mla_kernel,
        grid=(B, npair),
        in_specs=[
            pl.BlockSpec((None, L, 512), lambda b, p: (b, 0, COL_QLAT // 512)),
            pl.BlockSpec((None, L, MLA_KV_RANK), lambda b, p: (b, 0, COL_KVLAT // MLA_KV_RANK)),
            pl.BlockSpec((1, MLA_Q_RANK), const2),
            pl.BlockSpec((1, MLA_KV_RANK), const2),
            pl.BlockSpec((None, MLA_Q_RANK, MLA_PAIR * LANE), lambda b, p: (p, 0, 0)),
            pl.BlockSpec((None, MLA_KV_RANK, MLA_PAIR * LANE), lambda b, p: (p, 0, 0)),
            pl.BlockSpec((None, MLA_KV_RANK, MLA_PAIR * MLA_V), lambda b, p: (p, 0, 0)),
            pl.BlockSpec((L, LANE), const2),
            pl.BlockSpec((L, LANE), const2),
            pl.BlockSpec((L, LANE), const2),
            pl.BlockSpec((L, LANE), const2),
        ],
        out_specs=pl.BlockSpec((None, L, LANE), lambda b, p: (b, 0, p)),
        out_shape=jax.ShapeDtypeStruct((B, L, MLA_WIDTH), BF16),
        scratch_shapes=[
            pltpu.VMEM((L, MLA_Q_RANK), BF16),
            pltpu.VMEM((L, MLA_KV_RANK), BF16),
            pltpu.VMEM((L, LANE), F32),
            pltpu.VMEM((MLA_PAIR, L, LANE), BF16),
            pltpu.VMEM((MLA_PAIR, L, LANE), BF16),
            pltpu.VMEM((MLA_PAIR, L, LANE), BF16),
        ],
        compiler_params=_params("parallel", "arbitrary"),
        name="mla",
    )(u3, u3, gq[None, :], gkv[None, :], wq, wk, wv, cosq, cos, sin_a, sin_b)


def _gqa_kernel(q_ref, kv_ref, gq_ref, gk_ref, bd_ref, rep_ref, cos_ref, sa_ref, sb_ref, o_ref,
                q_s, k_s, v_s):
    L = q_ref.shape[0]
    tq = _attn_tq(L)
    half = GQA_HEAD_DIM // 4
    kw = GQA_KV_HEADS * GQA_HEAD_DIM

    def headnorm(x, bd, gain):
        ms = _dot2(x * x, bd) * (1.0 / GQA_HEAD_DIM)
        return x * lax.rsqrt(ms + NORM_EPS) * gain

    q = headnorm(q_ref[...].astype(F32), bd_ref[...], gq_ref[...])
    q = _rope(q, cos_ref[...], sa_ref[...], sb_ref[...], half) * (GQA_HEAD_DIM ** -0.5 * LOG2E)
    q_s[...] = q.astype(BF16)
    k = headnorm(kv_ref[:, 0:kw].astype(F32), bd_ref[0:kw, 0:kw], gk_ref[...])
    k = _rope(k, cos_ref[:, 0:kw], sa_ref[:, 0:kw], sb_ref[:, 0:kw], half)
    k_s[...] = _dot(k.astype(BF16), rep_ref[...]).astype(BF16)
    v_rep = _dot(kv_ref[:, kw:2 * kw], rep_ref[...]).astype(BF16)
    lane = lax.broadcasted_iota(jnp.int32, (L, GQA_GROUP_W), 1)
    for h in range(GQA_REP):
        v_s[h] = jnp.where(lane // GQA_HEAD_DIM == h, v_rep, jnp.zeros_like(v_rep))

    lane_q = lax.broadcasted_iota(jnp.int32, (tq, GQA_GROUP_W), 1)

    def load_q(h, rs):
        qb = q_s[rs, :]
        return jnp.where(lane_q // GQA_HEAD_DIM == h, qb, jnp.zeros_like(qb))

    def store(rs, o):
        o_ref[rs, :] = o.astype(o_ref.dtype)

    _attn_loop(L, GQA_REP, load_q, lambda h: k_s[...], lambda h: v_s[h], store)


def _gqa(u3, q_norm_g, k_norm_g):
    B, L, _ = u3.shape
    kw = GQA_KV_HEADS * GQA_HEAD_DIM
    gq = jnp.tile(q_norm_g, GQA_REP)[None, :]
    gk = jnp.tile(k_norm_g, GQA_KV_HEADS)[None, :]
    idx = np.arange(GQA_GROUP_W)
    bd = (idx[:, None] // GQA_HEAD_DIM == idx[None, :] // GQA_HEAD_DIM).astype(np.float32)
    rep = np.zeros((GQA_KV_HEADS, kw, GQA_GROUP_W), np.float32)
    for g in range(GQA_KV_HEADS):
        rep[g, g * GQA_HEAD_DIM + idx % GQA_HEAD_DIM, idx] = 1.0
    cos, sin_a, sin_b = _rope_tables(L, GQA_HEAD_DIM, GQA_GROUP_W, None)
    const2 = lambda b, g: (0, 0)
    return pl.pallas_call(
        _gqa_kernel,
        grid=(B, GQA_KV_HEADS),
        in_specs=[
            pl.BlockSpec((None, L, GQA_GROUP_W), lambda b, g: (b, 0, COL_QD // GQA_GROUP_W + g)),
            pl.BlockSpec((None, L, 2 * kw), lambda b, g: (b, 0, COL_KD // (2 * kw))),
            pl.BlockSpec((1, GQA_GROUP_W), const2),
            pl.BlockSpec((1, kw), const2),
            pl.BlockSpec((GQA_GROUP_W, GQA_GROUP_W), const2),
            pl.BlockSpec((None, kw, GQA_GROUP_W), lambda b, g: (g, 0, 0)),
            pl.BlockSpec((L, GQA_GROUP_W), const2),
            pl.BlockSpec((L, GQA_GROUP_W), const2),
            pl.BlockSpec((L, GQA_GROUP_W), const2),
        ],
        out_specs=pl.BlockSpec((None, L, GQA_GROUP_W), lambda b, g: (b, 0, g)),
        out_shape=jax.ShapeDtypeStruct((B, L, GQA_WIDTH), BF16),
        scratch_shapes=[
            pltpu.VMEM((L, GQA_GROUP_W), BF16),
            pltpu.VMEM((L, GQA_GROUP_W), BF16),
            pltpu.VMEM((GQA_REP, L, GQA_GROUP_W), BF16),
        ],
        compiler_params=_params("parallel", "parallel"),
        name="gqa",
    )(u3, u3, gq, gk, jnp.asarray(bd, BF16), jnp.asarray(rep, BF16), cos, sin_a, sin_b)


GLA_BLK = 2 * GLA_CHUNK


def _gla_kernel(q_ref, k_ref, v_ref, sm_ref, wg_ref, bg_ref, tri_ref, o_ref,
                glog_s, oacc_s, vt_s, st_s):
    L = q_ref.shape[0]
    Q = GLA_CHUNK
    nb = L // GLA_BLK

    def transpose_block(b, carry):
        rows = pl.ds(pl.multiple_of(b * GLA_BLK, GLA_BLK), GLA_BLK)
        vt_s[b] = v_ref[rows, :].astype(F32).T.astype(BF16)
        return carry

    lax.fori_loop(0, nb, transpose_block, 0, unroll=2 if nb % 2 == 0 else 1)

    sm = sm_ref[...]
    g_pre = _dot(jnp.concatenate([sm, sm], axis=1), wg_ref[...]) + bg_ref[...]
    glog_s[...] = (jnp.minimum(g_pre, 0.0) - jnp.log(1.0 + jnp.exp(-jnp.abs(g_pre)))) * (1.0 / GLA_TAU)

    li = lax.broadcasted_iota(jnp.int32, (Q, Q), 0)
    si = lax.broadcasted_iota(jnp.int32, (Q, Q), 1)
    lane = lax.broadcasted_iota(jnp.int32, (Q, GLA_QK), 1)
    lane_s = lax.broadcasted_iota(jnp.int32, (GLA_DV, GLA_QK), 1)
    k_zero = jnp.zeros((Q, GLA_QK), BF16)

    def chunk_step(blk, j, d, first):
        rs = pl.ds(pl.multiple_of(blk * GLA_BLK + j * Q, Q), Q)
        keep = (si <= li) if d == 0 else (si >= li)
        g_c = glog_s[rs, d * GLA_QK:(d + 1) * GLA_QK]
        g_hi, g_lo = _split(g_c)
        gcum = _dot(tri_ref[d], g_hi) + _dot(tri_ref[d], g_lo)
        yield
        g_last = gcum[Q - 1:Q, :] if d == 0 else gcum[0:1, :]
        q = q_ref[rs, :].astype(F32) * (GLA_DK ** -0.5)
        k = k_ref[rs, :].astype(F32)
        v = v_ref[rs, :]
        qg = (q * jnp.exp(gcum)).astype(BF16)
        kg = (k * jnp.exp(-gcum)).astype(BF16)
        k_end = (k * jnp.exp(g_last - gcum)).astype(BF16)
        k_pad = jnp.concatenate([k_end, k_zero] if j == 0 else [k_zero, k_end], axis=0)
        u_t = _dot(vt_s[blk], k_pad)
        qhs = [jnp.where(lane // GLA_DK == h, qg, jnp.zeros_like(qg)) for h in range(GLA_HEADS)]
        yield
        atts = [jnp.where(keep, _dot(qh, kg, NT), 0.0).astype(BF16) for qh in qhs]
        yield
        vss = [slice(h * GLA_DV, (h + 1) * GLA_DV) for h in range(GLA_HEADS)]
        o_intra = [_dot(att, v[:, vs]) for att, vs in zip(atts, vss)]
        yield
        s_prev = st_s[d]
        s_prev_b = s_prev.astype(BF16)
        s_new = jnp.exp(g_last) * s_prev
        for h, vs in enumerate(vss):
            o_h = o_intra[h] + _dot(qhs[h], s_prev_b, NT)
            if first:
                oacc_s[rs, vs] = o_h
            else:
                o_ref[rs, vs] = (oacc_s[rs, vs] + o_h).astype(o_ref.dtype)
            s_new = s_new + jnp.where(lane_s // GLA_DK == h, u_t[vs, :], 0.0)
        st_s[d] = s_new

    st_s[...] = jnp.zeros(st_s.shape, F32)
    assert nb % 2 == 0
    half = nb // 2

    unroll = 2 if half % 2 == 0 else 1

    def make(first, base):
        def body(i, carry):
            chains = []
            for u in range(unroll):
                b = base + i * unroll + u
                for j in range(2):
                    chains += [chunk_step(b, j, 0, first), chunk_step(nb - 1 - b, 1 - j, 1, first)]
            _interleave(chains)
            return carry
        return body

    lax.fori_loop(0, half // unroll, make(True, 0), 0)
    lax.fori_loop(0, half // unroll, make(False, half), 0)


def _gla(u3, w_gate_up, b_gate):
    B, L, _ = u3.shape
    wg = jnp.zeros((LANE, 2 * GLA_QK), F32)
    for d in range(2):
        r0 = SMALL_GLR + d * GLA_GATE_RANK
        wg = wg.at[r0:r0 + GLA_GATE_RANK, d * GLA_QK:(d + 1) * GLA_QK].set(w_gate_up[d])
    wg_hi, wg_lo = _split(wg)
    wg2 = jnp.concatenate([wg_hi, wg_lo], axis=0)
    bg = b_gate.reshape(1, 2 * GLA_QK)
    l_idx = np.arange(GLA_CHUNK)
    tri = np.stack([l_idx[None, :] <= l_idx[:, None], l_idx[None, :] >= l_idx[:, None]]).astype(np.float32)
    const2 = lambda b: (0, 0)
    return pl.pallas_call(
        _gla_kernel,
        grid=(B,),
        in_specs=[
            pl.BlockSpec((None, L, GLA_QK), lambda b: (b, 0, COL_QC // GLA_QK)),
            pl.BlockSpec((None, L, GLA_QK), lambda b: (b, 0, COL_KC // GLA_QK)),
            pl.BlockSpec((None, L, GLA_WIDTH), lambda b: (b, 0, COL_VC // GLA_WIDTH)),
            pl.BlockSpec((None, L, LANE), lambda b: (b, 0, COL_SMALL // LANE)),
            pl.BlockSpec((2 * LANE, 2 * GLA_QK), const2),
            pl.BlockSpec((1, 2 * GLA_QK), const2),
            pl.BlockSpec((2, GLA_CHUNK, GLA_CHUNK), lambda b: (0, 0, 0)),
        ],
        out_specs=pl.BlockSpec((None, L, GLA_WIDTH), lambda b: (b, 0, 0)),
        out_shape=jax.ShapeDtypeStruct((B, L, GLA_WIDTH), BF16),
        scratch_shapes=[
            pltpu.VMEM((L, 2 * GLA_QK), F32),
            pltpu.VMEM((L, GLA_WIDTH), F32),
            pltpu.VMEM((L // GLA_BLK, GLA_WIDTH, GLA_BLK), BF16),
            pltpu.VMEM((2, GLA_DV, GLA_QK), F32),
        ],
        compiler_params=_params("parallel"),
        name="gla",
    )(u3, u3, u3, u3, wg2, bg, jnp.asarray(tri, BF16))


def _merge_kernel(gm_ref, za_ref, zb_ref, zc_ref, zd_ref, ya_ref, ob_ref, oc_ref, od_ref, x_ref,
                  wa_ref, wb_ref, wc_ref, wd_ref, wo_ref, ga_ref, gc_ref, fg_ref, o_ref, *, final_norm):
    def silu_b(z):
        hz = z * 0.5
        return hz + hz * jnp.tanh(hz)

    def gated(o_ref_, z_ref_):
        return o_ref_[...] * silu_b(z_ref_[...])

    def gate_mix(k, y):
        g = gm_ref[:, k * D_MODEL:(k + 1) * D_MODEL]
        return (0.5 + 0.5 * jnp.tanh(g * 0.5)) * y.astype(BF16)

    ya = gated(ya_ref, za_ref).astype(F32)
    ms = jnp.mean(ya * ya, axis=-1, keepdims=True)
    ya = (ya * lax.rsqrt(ms + NORM_EPS) * ga_ref[...]).astype(BF16)
    mixed = gate_mix(0, _dot(ya, wa_ref[...]))

    yb = gated(ob_ref, zb_ref)
    mixed = mixed + gate_mix(1, _dot(yb, wb_ref[...]))

    oc = oc_ref[...].astype(F32)
    parts = []
    for h in range(GLA_HEADS):
        oh = oc[:, h * GLA_DV:(h + 1) * GLA_DV]
        ms = jnp.mean(oh * oh, axis=-1, keepdims=True)
        parts.append(oh * lax.rsqrt(ms + NORM_EPS))
    yc = (jnp.concatenate(parts, axis=1) * gc_ref[...]).astype(BF16) * silu_b(zc_ref[...])
    mixed = mixed + gate_mix(2, _dot(yc, wc_ref[...]))

    yd = gated(od_ref, zd_ref)
    mixed = mixed + gate_mix(3, _dot(yd, wd_ref[...]))

    x = x_ref[...] + _dot(mixed, wo_ref[...])
    if final_norm:
        ms = jnp.mean(x * x, axis=-1, keepdims=True)
        x = x * lax.rsqrt(ms + NORM_EPS) * fg_ref[...]
    o_ref[...] = x


def _merge(u2, ya, ob, oc, od, x2, wa, wb, wc, wd, wo, ga, gc, fg, final_norm):
    t = x2.shape[0]
    tm = min(512, t)
    row = lambda c: (lambda i: (i, c))
    const2 = lambda i: (0, 0)
    resident = pl.Buffered(1)
    return pl.pallas_call(
        functools.partial(_merge_kernel, final_norm=final_norm),
        grid=(t // tm,),
        in_specs=[
            pl.BlockSpec((tm, N_BRANCH * D_MODEL), row(COL_GMERGE // (N_BRANCH * D_MODEL))),
            pl.BlockSpec((tm, SSD_INNER), row(COL_ZA // SSD_INNER)),
            pl.BlockSpec((tm, MLA_WIDTH), row(COL_ZB // MLA_WIDTH)),
            pl.BlockSpec((tm, GLA_WIDTH), row(COL_ZC // GLA_WIDTH)),
            pl.BlockSpec((tm, GQA_WIDTH), row(COL_ZD // GQA_WIDTH)),
            pl.BlockSpec((tm, SSD_INNER), row(0)),
            pl.BlockSpec((tm, MLA_WIDTH), row(0)),
            pl.BlockSpec((tm, GLA_WIDTH), row(0)),
            pl.BlockSpec((tm, GQA_WIDTH), row(0)),
            pl.BlockSpec((tm, D_MODEL), row(0)),
            pl.BlockSpec((SSD_INNER, D_MODEL), const2, pipeline_mode=resident),
            pl.BlockSpec((MLA_WIDTH, D_MODEL), const2, pipeline_mode=resident),
            pl.BlockSpec((GLA_WIDTH, D_MODEL), const2, pipeline_mode=resident),
            pl.BlockSpec((GQA_WIDTH, D_MODEL), const2, pipeline_mode=resident),
            pl.BlockSpec((D_MODEL, D_MODEL), const2, pipeline_mode=resident),
            pl.BlockSpec((1, SSD_INNER), const2),
            pl.BlockSpec((1, GLA_WIDTH), const2),
            pl.BlockSpec((1, D_MODEL), const2),
        ],
        out_specs=pl.BlockSpec((tm, D_MODEL), row(0)),
        out_shape=jax.ShapeDtypeStruct((t, D_MODEL), F32),
        compiler_params=_params("parallel"),
        name="merge",
    )(u2, u2, u2, u2, u2, ya, ob, oc, od, x2, wa, wb, wc, wd, wo, ga, gc, fg)


def _reorder_w_in(w):
    split_idx = [int(v) for v in np.cumsum(IN_WIDTHS)[:-1]]
    (g_merge, z_a, xbc, dt_raw, z_b, q_lat, kv_lat, k_rope, z_c, q_c, k_c, v_c, g_lr,
     z_d, q_d, k_d, v_d) = jnp.split(w, split_idx, axis=1)
    xs, bm, cm = xbc[:, :SSD_INNER], xbc[:, SSD_INNER:SSD_INNER + SSD_BC], xbc[:, SSD_INNER + SSD_BC:]
    small = jnp.concatenate([dt_raw, g_lr, k_rope, jnp.zeros((w.shape[0], LANE - 96), w.dtype)], axis=1)
    cols = [g_merge, z_a, xs, z_b, z_c, z_d, q_d, v_c, q_lat, small, bm, cm, kv_lat, q_c, k_c, k_d, v_d]
    out = jnp.concatenate(cols, axis=1)
    assert out.shape[1] == N_PROJ
    return out.astype(BF16)


def kernel(x, norm_g, w_in, conv_w, conv_b, a_log, dt_bias, d_skip, ssd_norm_g, q_lat_norm_g, kv_lat_norm_g, w_q_b, w_kv_b, w_gate_up, b_gate, gla_norm_g, q_norm_g, k_norm_g, w_br_a, w_br_b, w_br_c, w_br_d, w_out, final_g):
    B, L, D = x.shape
    depth = w_in.shape[0]
    x2 = x.reshape(B * L, D)
    for i in range(depth):
        u2 = _inproj(x2, norm_g[i][None, :], _reorder_w_in(w_in[i]))
        u3 = u2.reshape(B, L, N_PROJ)
        ya = _ssd(u3, conv_w[i], conv_b[i], a_log[i], dt_bias[i], d_skip[i])
        ob = _mla(u3, q_lat_norm_g[i], kv_lat_norm_g[i], w_q_b[i], w_kv_b[i])
        oc = _gla(u3, w_gate_up[i], b_gate[i])
        od = _gqa(u3, q_norm_g[i], k_norm_g[i])
        x2 = _merge(
            u2, ya.reshape(B * L, -1), ob.reshape(B * L, -1), oc.reshape(B * L, -1), od.reshape(B * L, -1), x2,
            w_br_a[i].astype(BF16), w_br_b[i].astype(BF16), w_br_c[i].astype(BF16), w_br_d[i].astype(BF16),
            w_out[i].astype(BF16), ssd_norm_g[i][None, :], gla_norm_g[i][None, :], final_g[None, :],
            final_norm=(i == depth - 1))
    return x2.reshape(B, L, D)
```

```python
import functools

import numpy as np
import jax
import jax.numpy as jnp
from jax import lax
from jax.experimental import pallas as pl
from jax.experimental.pallas import tpu as pltpu

F32 = jnp.float32
BF16 = jnp.bfloat16

D_MODEL = 1024
GRID_W = 64
ROPE_THETA = 10000.0
NORM_EPS = 1e-6
N_BRANCH = 4

SSD_HEADS = 16
SSD_HEAD_DIM = 64
SSD_INNER = SSD_HEADS * SSD_HEAD_DIM
SSD_GROUPS = 2
SSD_STATE = 128
SSD_CONV = 5
SSD_CHUNK = 128
SSD_BC = SSD_GROUPS * SSD_STATE

MLA_HEADS = 8
MLA_Q_RANK = 384
MLA_KV_RANK = 256
MLA_NOPE = 64
MLA_ROPE = 32
MLA_V = 64
MLA_WIDTH = MLA_HEADS * MLA_V

GLA_HEADS = 4
GLA_DK = 64
GLA_DV = 128
GLA_GATE_RANK = 16
GLA_TAU = 16.0
GLA_CHUNK = 64
GLA_WIDTH = GLA_HEADS * GLA_DV
GLA_QK = GLA_HEADS * GLA_DK

GQA_HEADS = 8
GQA_KV_HEADS = 2
GQA_HEAD_DIM = 64
GQA_WIDTH = GQA_HEADS * GQA_HEAD_DIM
GQA_REP = GQA_HEADS // GQA_KV_HEADS
GQA_GROUP_W = GQA_REP * GQA_HEAD_DIM

IN_WIDTHS = (
    N_BRANCH * D_MODEL, SSD_INNER, SSD_INNER + 2 * SSD_BC, 2 * SSD_HEADS, MLA_WIDTH, MLA_Q_RANK,
    MLA_KV_RANK, MLA_ROPE, GLA_WIDTH, GLA_QK, GLA_QK, GLA_WIDTH, 2 * GLA_GATE_RANK, GQA_WIDTH,
    GQA_WIDTH, GQA_KV_HEADS * GQA_HEAD_DIM, GQA_KV_HEADS * GQA_HEAD_DIM,
)

LANE = 128

SMALL_DT = 0
SMALL_GLR = 32
SMALL_KROPE = 64
COL_GMERGE = 0
COL_ZA = 4096
COL_XS = 5120
COL_ZB = 6144
COL_ZC = 6656
COL_ZD = 7168
COL_QD = 7680
COL_VC = 8192
COL_QLAT = 8704
COL_SMALL = COL_QLAT + MLA_Q_RANK
COL_BM = 9216
COL_CM = 9472
COL_KVLAT = 9728
COL_QC = 9984
COL_KC = 10240
COL_KD = 10496
N_PROJ = 10752

VMEM_LIMIT = 56 * 1024 * 1024

NN = ((1,), (0,))
NT = ((1,), (1,))
TN = ((0,), (0,))


def _dot(a, b, dims=NN):
    return lax.dot_general(a, b, (dims, ((), ())), preferred_element_type=F32)


def _split(a):
    hi = a.astype(BF16)
    lo = (a - hi.astype(F32)).astype(BF16)
    return hi, lo


def _softplus(x):
    return jnp.maximum(x, 0.0) + jnp.log1p(jnp.exp(-jnp.abs(x)))


def _silu(x):
    return x * (1.0 / (1.0 + jnp.exp(-x)))


def _params(*sem):
    return pltpu.CompilerParams(dimension_semantics=sem, vmem_limit_bytes=VMEM_LIMIT)


def _interleave(gens):
    gens = list(gens)
    while gens:
        alive = []
        for g in gens:
            try:
                next(g)
                alive.append(g)
            except StopIteration:
                pass
        gens = alive


def _both_directions(nc, chunk_step, unroll=1):
    assert nc % 2 == 0
    half = nc // 2
    if half % unroll:
        unroll = 1

    def make(first, base):
        def body(i, carry):
            chains = []
            for u in range(unroll):
                c = base + i * unroll + u
                chains += [chunk_step(c, 0, first), chunk_step(nc - 1 - c, 1, first)]
            _interleave(chains)
            return carry
        return body

    lax.fori_loop(0, half // unroll, make(True, 0), 0)
    lax.fori_loop(0, half // unroll, make(False, half), 0)


def _inproj_kernel(x_ref, g_ref, w_ref, o_ref, h_ref):
    @pl.when(pl.program_id(1) == 0)
    def _():
        x = x_ref[...]
        ms = jnp.mean(x * x, axis=-1, keepdims=True)
        h_ref[...] = (x * lax.rsqrt(ms + NORM_EPS) * g_ref[...]).astype(BF16)

    o_ref[...] = _dot(h_ref[...], w_ref[...]).astype(BF16)


def _inproj(x2, g, w, layer):
    t = x2.shape[0]
    tm = min(1024, t)
    tn = 1536
    return pl.pallas_call(
        _inproj_kernel,
        grid=(t // tm, N_PROJ // tn),
        in_specs=[
            pl.BlockSpec((tm, D_MODEL), lambda i, j: (i, 0)),
            pl.BlockSpec((None, 1, D_MODEL), lambda i, j: (layer, 0, 0)),
            pl.BlockSpec((None, D_MODEL, tn), lambda i, j: (layer, 0, j)),
        ],
        out_specs=pl.BlockSpec((tm, tn), lambda i, j: (i, j)),
        out_shape=jax.ShapeDtypeStruct((t, N_PROJ), BF16),
        scratch_shapes=[pltpu.VMEM((tm, D_MODEL), BF16)],
        compiler_params=_params("parallel", "arbitrary"),
        name="inproj",
    )(x2, g, w)


CONV_W = 256
CONV_HALO = 8


def _ssd_kernel(xs_ref, bm_ref, cm_ref, sm_ref, cwx_ref, cwb_ref, cwc_ref, cbx_ref, cbb_ref, cbc_ref,
                dtb_ref, aneg_ref, dskip_ref, tri_ref, exp_ref, o_ref,
                pad_s, xs_s, bm_s, cm_s, dt_s, yacc_s, st_s):
    L = xs_ref.shape[0]
    Q = SSD_CHUNK
    nc = L // Q
    ext = Q + 2 * CONV_HALO

    pad_s[0:CONV_HALO, :] = jnp.zeros((CONV_HALO, CONV_W), F32)
    pad_s[L + CONV_HALO:L + 2 * CONV_HALO, :] = jnp.zeros((CONV_HALO, CONV_W), F32)

    def conv_group(src_ref, w_ref, b_ref, col, dst_ref, dst_dtype):
        pad_s[CONV_HALO:L + CONV_HALO, :] = src_ref[:, col:col + CONV_W].astype(F32)
        w = w_ref[:, col:col + CONV_W]
        b = b_ref[:, col:col + CONV_W]

        def body(r, carry):
            base = pl.multiple_of(r * Q, Q)
            win = pad_s[pl.ds(base, ext), :]
            acc = jnp.zeros((Q, CONV_W), F32) + b
            for k in range(SSD_CONV):
                shift = (SSD_CONV // 2 - k) % ext
                sh = win if shift == 0 else pltpu.roll(win, shift, 0)
                acc = acc + sh[CONV_HALO:CONV_HALO + Q, :] * w[k:k + 1, :]
            dst_ref[pl.ds(base, Q), col:col + CONV_W] = _silu(acc).astype(dst_dtype)
            return carry

        lax.fori_loop(0, nc, body, 0)

    for cg in range(SSD_INNER // CONV_W):
        conv_group(xs_ref, cwx_ref, cbx_ref, cg * CONV_W, xs_s, F32)
    conv_group(bm_ref, cwb_ref, cbb_ref, 0, bm_s, BF16)
    conv_group(cm_ref, cwc_ref, cbc_ref, 0, cm_s, BF16)

    dt_s[...] = _softplus(sm_ref[...].astype(F32) + dtb_ref[...])

    li = lax.broadcasted_iota(jnp.int32, (Q, Q), 0)
    si = lax.broadcasted_iota(jnp.int32, (Q, Q), 1)
    lane = lax.broadcasted_iota(jnp.int32, (Q, SSD_INNER), 1)
    low_half = (lane % LANE) < SSD_HEAD_DIM
    gw = SSD_INNER // SSD_GROUPS
    heads_per_group = SSD_HEADS // SSD_GROUPS

    def chunk_step(c, d, first):
        rs = pl.ds(pl.multiple_of(c * Q, Q), Q)
        keep = (si <= li) if d == 0 else (si >= li)
        tri = tri_ref[d]
        dt_c = dt_s[rs, :]
        da_c = dt_c * aneg_ref[...]
        da_hi, da_lo = _split(da_c)
        ac = _dot(tri, da_hi) + _dot(tri, da_lo)
        tri_t = tri_ref[1 - d]
        ac_t = _dot(da_hi, tri_t, TN) + _dot(da_lo, tri_t, TN)
        yield
        ac_hi, ac_lo = _split(ac)
        ac_x = _dot(jnp.concatenate([ac_hi, ac_lo], axis=1), exp_ref[d])
        dt_hi, dt_lo = _split(dt_c)
        dt_x = _dot(jnp.concatenate([dt_hi, dt_lo], axis=1), exp_ref[d])
        cm_c = cm_s[rs, :]
        bm_c = bm_s[rs, :]
        cmgs = [cm_c[:, g * SSD_STATE:(g + 1) * SSD_STATE] for g in range(SSD_GROUPS)]
        bmgs = [bm_c[:, g * SSD_STATE:(g + 1) * SSD_STATE] for g in range(SSD_GROUPS)]
        cbs = [_dot(cmg, bmg, NT) for cmg, bmg in zip(cmgs, bmgs)]
        yield
        xs_c = xs_s[rs, :]
        xd = xs_c * dt_x
        xd_b = xd.astype(BF16)
        a_last = ac_x[Q - 1:Q, :] if d == 0 else ac_x[0:1, :]
        xdd = (xd * jnp.exp(a_last - ac_x)).astype(BF16)
        eac = jnp.exp(ac_x)
        xd_lo = jnp.where(low_half, xd_b, jnp.zeros_like(xd_b))
        xd_hi = jnp.where(low_half, jnp.zeros_like(xd_b), xd_b)
        st_news = [_dot(bmgs[g], xdd[:, g * gw:(g + 1) * gw], TN) for g in range(SSD_GROUPS)]
        yield
        y_diag = {}
        for g in range(SSD_GROUPS):
            for pair in range(heads_per_group // 2):
                c0 = g * gw + pair * LANE
                lms = []
                for q in range(2):
                    e = d * SSD_HEADS + g * heads_per_group + 2 * pair + q
                    seg = ac[:, e:e + 1] - ac_t[e:e + 1, :]
                    lms.append((jnp.where(keep, jnp.exp(seg), 0.0) * cbs[g]).astype(BF16))
                x_pair = jnp.concatenate([xd_lo[:, c0:c0 + LANE], xd_hi[:, c0:c0 + LANE]], axis=0)
                y = _dot(jnp.concatenate(lms, axis=1), x_pair)
                if d == 0:
                    y = y + xs_c[:, c0:c0 + LANE] * dskip_ref[:, c0:c0 + LANE]
                y_diag[c0] = y
            yield
        for g in range(SSD_GROUPS):
            s_prev = st_s[d, g]
            y_off = _dot(cmgs[g], s_prev.astype(BF16)) * eac[:, g * gw:(g + 1) * gw]
            st_s[d, g] = jnp.exp(a_last[:, g * gw:(g + 1) * gw]) * s_prev + st_news[g]
            for pair in range(heads_per_group // 2):
                c0 = g * gw + pair * LANE
                y_pair = y_diag[c0] + y_off[:, pair * LANE:(pair + 1) * LANE]
                if first:
                    yacc_s[rs, c0:c0 + LANE] = y_pair
                else:
                    o_ref[rs, c0:c0 + LANE] = (yacc_s[rs, c0:c0 + LANE] + y_pair).astype(o_ref.dtype)

    st_s[...] = jnp.zeros(st_s.shape, F32)
    _both_directions(nc, chunk_step, unroll=2)


def _ssd(u3, conv_w, conv_b, a_log, dt_bias, d_skip):
    B, L, _ = u3.shape
    cwx, cwb, cwc = conv_w[:, :SSD_INNER], conv_w[:, SSD_INNER:SSD_INNER + SSD_BC], conv_w[:, SSD_INNER + SSD_BC:]
    cb2 = conv_b[None, :]
    cbx, cbb, cbc = cb2[:, :SSD_INNER], cb2[:, SSD_INNER:SSD_INNER + SSD_BC], cb2[:, SSD_INNER + SSD_BC:]
    zpad = jnp.zeros((LANE - 2 * SSD_HEADS,), F32)
    dtb = jnp.concatenate([dt_bias.reshape(-1), zpad])[None, :]
    aneg = jnp.concatenate([-jnp.exp(a_log.reshape(-1)), zpad])[None, :]
    dskip = jnp.repeat(d_skip, SSD_HEAD_DIM)[None, :]
    l_idx = np.arange(SSD_CHUNK)
    tri = np.stack([l_idx[None, :] <= l_idx[:, None], l_idx[None, :] >= l_idx[:, None]]).astype(np.float32)
    expm = np.zeros((2, 2 * LANE, SSD_INNER), np.float32)
    for d in range(2):
        for e in range(SSD_HEADS):
            for part in range(2):
                expm[d, part * LANE + d * SSD_HEADS + e, e * SSD_HEAD_DIM:(e + 1) * SSD_HEAD_DIM] = 1.0
    const2 = lambda b: (0, 0)
    const3 = lambda b: (0, 0, 0)
    return pl.pallas_call(
        _ssd_kernel,
        grid=(B,),
        in_specs=[
            pl.BlockSpec((None, L, SSD_INNER), lambda b: (b, 0, COL_XS // SSD_INNER)),
            pl.BlockSpec((None, L, SSD_BC), lambda b: (b, 0, COL_BM // SSD_BC)),
            pl.BlockSpec((None, L, SSD_BC), lambda b: (b, 0, COL_CM // SSD_BC)),
            pl.BlockSpec((None, L, LANE), lambda b: (b, 0, COL_SMALL // LANE)),
            pl.BlockSpec((SSD_CONV, SSD_INNER), const2),
            pl.BlockSpec((SSD_CONV, SSD_BC), const2),
            pl.BlockSpec((SSD_CONV, SSD_BC), const2),
            pl.BlockSpec((1, SSD_INNER), const2),
            pl.BlockSpec((1, SSD_BC), const2),
            pl.BlockSpec((1, SSD_BC), const2),
            pl.BlockSpec((1, LANE), const2),
            pl.BlockSpec((1, LANE), const2),
            pl.BlockSpec((1, SSD_INNER), const2),
            pl.BlockSpec((2, SSD_CHUNK, SSD_CHUNK), const3),
            pl.BlockSpec((2, 2 * LANE, SSD_INNER), const3),
        ],
        out_specs=pl.BlockSpec((None, L, SSD_INNER), lambda b: (b, 0, 0)),
        out_shape=jax.ShapeDtypeStruct((B, L, SSD_INNER), BF16),
        scratch_shapes=[
            pltpu.VMEM((L + 2 * CONV_HALO, CONV_W), F32),
            pltpu.VMEM((L, SSD_INNER), F32),
            pltpu.VMEM((L, SSD_BC), BF16),
            pltpu.VMEM((L, SSD_BC), BF16),
            pltpu.VMEM((L, LANE), F32),
            pltpu.VMEM((L, SSD_INNER), F32),
            pltpu.VMEM((2, SSD_GROUPS, SSD_STATE, SSD_INNER // SSD_GROUPS), F32),
        ],
        compiler_params=_params("parallel"),
        name="ssd",
    )(u3, u3, u3, u3, cwx, cwb, cwc, cbx, cbb, cbc, dtb, aneg, dskip,
      jnp.asarray(tri, BF16), jnp.asarray(expm, BF16))


LOG2E = 1.4426950408889634


def _rope(x, cos, sin_a, sin_b, half):
    w = x.shape[-1]
    return x * cos + pltpu.roll(x, w - half, 1) * sin_a + pltpu.roll(x, half, 1) * sin_b


ATTN_TQ = 256
ATTN_CHAINS = 16


def _attn_tq(L):
    return ATTN_TQ if L % ATTN_TQ == 0 else L


def _attend_chain(load_q, load_k, load_v, finish):
    s = _dot(load_q(), load_k(), NT)
    yield
    m = jnp.max(s, axis=-1, keepdims=True)
    p = jnp.exp2(s - m)
    r = 1.0 / jnp.sum(p, axis=-1, keepdims=True)
    pb = p.astype(BF16)
    yield
    finish(_dot(pb, load_v()) * r)


def _skewed(gens):
    pending = list(gens)
    active = []
    while pending or active:
        if pending:
            active.append(pending.pop(0))
        alive = []
        for g in active:
            try:
                next(g)
                alive.append(g)
            except StopIteration:
                pass
        active = alive


def _attn_loop(L, heads, load_q, load_k, load_v, store):
    tq = _attn_tq(L)
    n = L // tq
    tiles = ATTN_CHAINS // heads
    if n % tiles:
        tiles = 1

    def body(i, carry):
        chains = []
        for u in range(tiles):
            rs = pl.ds(pl.multiple_of((i * tiles + u) * tq, tq), tq)
            acc = []
            for h in range(heads):
                def finish(o, acc=acc, rs=rs, last=(h == heads - 1)):
                    acc.append(o)
                    if last:
                        store(rs, functools.reduce(lambda a, b: a + b, acc))
                chains.append(_attend_chain(functools.partial(load_q, h, rs), functools.partial(load_k, h),
                                            functools.partial(load_v, h), finish))
        _skewed(chains)
        return carry

    lax.fori_loop(0, n // tiles, body, 0)


MLA_PAIR = 2


def _mla_kernel(ql_ref, kv_ref, gq_ref, gkv_ref, wq_ref, wk_ref, wv_ref, cosq_ref, cosk_ref, sa_ref, sb_ref,
                o_ref, qn_s, kvn_s, kr_s, q_s, k_s, v_s):
    L = ql_ref.shape[0]
    tq = _attn_tq(L)
    half = MLA_ROPE // 4

    @pl.when(pl.program_id(1) == 0)
    def _():
        ql = ql_ref[:, 0:MLA_Q_RANK].astype(F32)
        ms = jnp.mean(ql * ql, axis=-1, keepdims=True)
        qn_s[...] = (ql * lax.rsqrt(ms + NORM_EPS) * gq_ref[...]).astype(BF16)
        kv = kv_ref[...].astype(F32)
        ms = jnp.mean(kv * kv, axis=-1, keepdims=True)
        kvn_s[...] = (kv * lax.rsqrt(ms + NORM_EPS) * gkv_ref[...]).astype(BF16)
        sm = ql_ref[:, MLA_Q_RANK:MLA_Q_RANK + LANE].astype(F32)
        kr_s[...] = _rope(sm, cosk_ref[...], sa_ref[...], sb_ref[...], half)

    scale = (MLA_NOPE + MLA_ROPE) ** -0.5 * LOG2E
    q2 = _dot(qn_s[...], wq_ref[...])
    k2 = _dot(kvn_s[...], wk_ref[...])
    v2 = _dot(kvn_s[...], wv_ref[...]).astype(BF16)
    lane_v = lax.broadcasted_iota(jnp.int32, (L, LANE), 1)
    for h in range(MLA_PAIR):
        hs = slice(h * LANE, (h + 1) * LANE)
        q = _rope(q2[:, hs], cosq_ref[...], sa_ref[...], sb_ref[...], half) * scale
        q_s[h] = q.astype(BF16)
        k_s[h] = (k2[:, hs] + kr_s[...]).astype(BF16)
        v_s[h] = jnp.where(lane_v // MLA_V == h, v2, jnp.zeros_like(v2))

    def store(rs, o):
        o_ref[rs, :] = o.astype(o_ref.dtype)

    _attn_loop(L, MLA_PAIR, lambda h, rs: q_s[h, rs, :], lambda h: k_s[h], lambda h: v_s[h], store)


def _rope_tables(L, d_rot, width, lane0):
    rows = L // GRID_W
    row = jnp.repeat(jnp.arange(rows), GRID_W).astype(F32)
    col = jnp.tile(jnp.arange(GRID_W), rows).astype(F32)
    m = d_rot // 2
    inv = ROPE_THETA ** (-jnp.arange(0, m, 2, dtype=F32) / m)
    ang_r = row[:, None] * inv
    ang_c = col[:, None] * inv
    ang = jnp.concatenate([ang_r, ang_r, ang_c, ang_c], axis=-1)
    cos, sin = jnp.cos(ang), jnp.sin(ang)
    first = (jnp.arange(d_rot) % m) < (m // 2)
    sin_a = jnp.where(first, -sin, 0.0)
    sin_b = jnp.where(first, 0.0, sin)
    if lane0 is None:
        reps = width // d_rot
        return tuple(jnp.tile(t, (1, reps)) for t in (cos, sin_a, sin_b))
    padl, padr = lane0, width - lane0 - d_rot
    return tuple(jnp.pad(t, ((0, 0), (padl, padr))) for t in (cos, sin_a, sin_b))


def _mla(u3, gq, gkv, w_q_b, w_kv_b):
    B, L, _ = u3.shape
    npair = MLA_HEADS // MLA_PAIR
    dq = MLA_NOPE + MLA_ROPE
    wq = w_q_b.reshape(MLA_Q_RANK, MLA_HEADS, dq).transpose(1, 0, 2)
    wq = jnp.pad(wq, ((0, 0), (0, 0), (0, LANE - dq))).astype(BF16)
    wkv = w_kv_b.reshape(MLA_KV_RANK, MLA_HEADS, MLA_NOPE + MLA_V).transpose(1, 0, 2)
    wk = jnp.pad(wkv[..., :MLA_NOPE], ((0, 0), (0, 0), (0, LANE - MLA_NOPE))).astype(BF16)
    wv = wkv[..., MLA_NOPE:].astype(BF16)

    def by_pair(w):
        h, k, n = w.shape
        return w.reshape(npair, MLA_PAIR, k, n).transpose(0, 2, 1, 3).reshape(npair, k, MLA_PAIR * n)

    wq, wk, wv = by_pair(wq), by_pair(wk), by_pair(wv)
    cos, sin_a, sin_b = _rope_tables(L, MLA_ROPE, LANE, SMALL_KROPE)
    cosq = cos.at[:, :MLA_NOPE].set(1.0)
    const2 = lambda b, p: (0, 0)
    return pl.pallas_call(
        _mla_kernel,
        grid=(B, npair),
        in_specs=[
            pl.BlockSpec((None, L, 512), lambda b, p: (b, 0, COL_QLAT // 512)),
            pl.BlockSpec((None, L, MLA_KV_RANK), lambda b, p: (b, 0, COL_KVLAT // MLA_KV_RANK)),
            pl.BlockSpec((1, MLA_Q_RANK), const2),
            pl.BlockSpec((1, MLA_KV_RANK), const2),
            pl.BlockSpec((None, MLA_Q_RANK, MLA_PAIR * LANE), lambda b, p: (p, 0, 0)),
            pl.BlockSpec((None, MLA_KV_RANK, MLA_PAIR * LANE), lambda b, p: (p, 0, 0)),
            pl.BlockSpec((None, MLA_KV_RANK, MLA_PAIR * MLA_V), lambda b, p: (p, 0, 0)),
            pl.BlockSpec((L, LANE), const2),
            pl.BlockSpec((L, LANE), const2),
            pl.BlockSpec((L, LANE), const2),
            pl.BlockSpec((L, LANE), const2),
        ],
        out_specs=pl.BlockSpec((None, L, LANE), lambda b, p: (b, 0, p)),
        out_shape=jax.ShapeDtypeStruct((B, L, MLA_WIDTH), BF16),
        scratch_shapes=[
            pltpu.VMEM((L, MLA_Q_RANK), BF16),
            pltpu.VMEM((L, MLA_KV_RANK), BF16),
            pltpu.VMEM((L, LANE), F32),
            pltpu.VMEM((MLA_PAIR, L, LANE), BF16),
            pltpu.VMEM((MLA_PAIR, L, LANE), BF16),
            pltpu.VMEM((MLA_PAIR, L, LANE), BF16),
        ],
        compiler_params=_params("parallel", "arbitrary"),
        name="mla",
    )(u3, u3, gq[None, :], gkv[None, :], wq, wk, wv, cosq, cos, sin_a, sin_b)


def _gqa_kernel(q_ref, kv_ref, gq_ref, gk_ref, bd_ref, rep_ref, cos_ref, sa_ref, sb_ref, o_ref,
                q_s, k_s, v_s):
    L = q_ref.shape[0]
    tq = _attn_tq(L)
    half = GQA_HEAD_DIM // 4
    kw = GQA_KV_HEADS * GQA_HEAD_DIM

    def headnorm(x, bd, gain):
        ms = _dot((x * x).astype(BF16), bd) * (1.0 / GQA_HEAD_DIM)
        return x * lax.rsqrt(ms + NORM_EPS) * gain

    q = headnorm(q_ref[...].astype(F32), bd_ref[...], gq_ref[...])
    q = _rope(q, cos_ref[...], sa_ref[...], sb_ref[...], half)
    q_s[...] = q.astype(BF16)
    k = headnorm(kv_ref[:, 0:kw].astype(F32), bd_ref[0:kw, 0:kw], gk_ref[...])
    k = _rope(k, cos_ref[:, 0:kw], sa_ref[:, 0:kw], sb_ref[:, 0:kw], half)
    k_s[...] = _dot(k.astype(BF16), rep_ref[...]).astype(BF16)
    v_rep = _dot(kv_ref[:, kw:2 * kw], rep_ref[...]).astype(BF16)
    lane = lax.broadcasted_iota(jnp.int32, (L, GQA_GROUP_W), 1)
    for h in range(GQA_REP):
        v_s[h] = jnp.where(lane // GQA_HEAD_DIM == h, v_rep, jnp.zeros_like(v_rep))

    lane_q = lax.broadcasted_iota(jnp.int32, (tq, GQA_GROUP_W), 1)

    def load_q(h, rs):
        qb = q_s[rs, :]
        return jnp.where(lane_q // GQA_HEAD_DIM == h, qb, jnp.zeros_like(qb))

    def store(rs, o):
        o_ref[rs, :] = o.astype(o_ref.dtype)

    _attn_loop(L, GQA_REP, load_q, lambda h: k_s[...], lambda h: v_s[h], store)


def _gqa(u3, q_norm_g, k_norm_g):
    B, L, _ = u3.shape
    kw = GQA_KV_HEADS * GQA_HEAD_DIM
    gq = jnp.tile(q_norm_g, GQA_REP)[None, :] * (GQA_HEAD_DIM ** -0.5 * LOG2E)
    gk = jnp.tile(k_norm_g, GQA_KV_HEADS)[None, :]
    idx = np.arange(GQA_GROUP_W)
    bd = (idx[:, None] // GQA_HEAD_DIM == idx[None, :] // GQA_HEAD_DIM).astype(np.float32)
    rep = np.zeros((GQA_KV_HEADS, kw, GQA_GROUP_W), np.float32)
    for g in range(GQA_KV_HEADS):
        rep[g, g * GQA_HEAD_DIM + idx % GQA_HEAD_DIM, idx] = 1.0
    cos, sin_a, sin_b = _rope_tables(L, GQA_HEAD_DIM, GQA_GROUP_W, None)
    const2 = lambda b, g: (0, 0)
    return pl.pallas_call(
        _gqa_kernel,
        grid=(B, GQA_KV_HEADS),
        in_specs=[
            pl.BlockSpec((None, L, GQA_GROUP_W), lambda b, g: (b, 0, COL_QD // GQA_GROUP_W + g)),
            pl.BlockSpec((None, L, 2 * kw), lambda b, g: (b, 0, COL_KD // (2 * kw))),
            pl.BlockSpec((1, GQA_GROUP_W), const2),
            pl.BlockSpec((1, kw), const2),
            pl.BlockSpec((GQA_GROUP_W, GQA_GROUP_W), const2),
            pl.BlockSpec((None, kw, GQA_GROUP_W), lambda b, g: (g, 0, 0)),
            pl.BlockSpec((L, GQA_GROUP_W), const2),
            pl.BlockSpec((L, GQA_GROUP_W), const2),
            pl.BlockSpec((L, GQA_GROUP_W), const2),
        ],
        out_specs=pl.BlockSpec((None, L, GQA_GROUP_W), lambda b, g: (b, 0, g)),
        out_shape=jax.ShapeDtypeStruct((B, L, GQA_WIDTH), BF16),
        scratch_shapes=[
            pltpu.VMEM((L, GQA_GROUP_W), BF16),
            pltpu.VMEM((L, GQA_GROUP_W), BF16),
            pltpu.VMEM((GQA_REP, L, GQA_GROUP_W), BF16),
        ],
        compiler_params=_params("parallel", "parallel"),
        name="gqa",
    )(u3, u3, gq, gk, jnp.asarray(bd, BF16), jnp.asarray(rep, BF16), cos, sin_a, sin_b)


GLA_BLK = 2 * GLA_CHUNK


def _gla_kernel(q_ref, k_ref, v_ref, sm_ref, wg_ref, bg_ref, tri_ref, o_ref,
                glog_s, oacc_s, vt_s, st_s):
    L = q_ref.shape[0]
    Q = GLA_CHUNK
    nb = L // GLA_BLK

    def transpose_block(b, carry):
        rows = pl.ds(pl.multiple_of(b * GLA_BLK, GLA_BLK), GLA_BLK)
        vt_s[b] = v_ref[rows, :].astype(F32).T.astype(BF16)
        return carry

    lax.fori_loop(0, nb, transpose_block, 0, unroll=2 if nb % 2 == 0 else 1)

    sm = sm_ref[...]
    g_pre = _dot(jnp.concatenate([sm, sm], axis=1), wg_ref[...]) + bg_ref[...]
    glog_s[...] = (jnp.minimum(g_pre, 0.0) - jnp.log(1.0 + jnp.exp(-jnp.abs(g_pre)))) * (1.0 / GLA_TAU)

    li = lax.broadcasted_iota(jnp.int32, (Q, Q), 0)
    si = lax.broadcasted_iota(jnp.int32, (Q, Q), 1)
    lane = lax.broadcasted_iota(jnp.int32, (Q, GLA_QK), 1)
    lane_s = lax.broadcasted_iota(jnp.int32, (GLA_DV, GLA_QK), 1)
    k_zero = jnp.zeros((Q, GLA_QK), BF16)

    def chunk_step(blk, j, d, first):
        rs = pl.ds(pl.multiple_of(blk * GLA_BLK + j * Q, Q), Q)
        keep = (si <= li) if d == 0 else (si >= li)
        g_c = glog_s[rs, d * GLA_QK:(d + 1) * GLA_QK]
        g_hi, g_lo = _split(g_c)
        gcum = _dot(tri_ref[d], g_hi) + _dot(tri_ref[d], g_lo)
        yield
        g_last = gcum[Q - 1:Q, :] if d == 0 else gcum[0:1, :]
        q = q_ref[rs, :].astype(F32) * (GLA_DK ** -0.5)
        k = k_ref[rs, :].astype(F32)
        v = v_ref[rs, :]
        qg = (q * jnp.exp(gcum)).astype(BF16)
        kg = (k * jnp.exp(-gcum)).astype(BF16)
        k_end = (k * jnp.exp(g_last - gcum)).astype(BF16)
        k_pad = jnp.concatenate([k_end, k_zero] if j == 0 else [k_zero, k_end], axis=0)
        u_t = _dot(vt_s[blk], k_pad)
        qhs = [jnp.where(lane // GLA_DK == h, qg, jnp.zeros_like(qg)) for h in range(GLA_HEADS)]
        yield
        atts = [jnp.where(keep, _dot(qh, kg, NT), 0.0).astype(BF16) for qh in qhs]
        yield
        vss = [slice(h * GLA_DV, (h + 1) * GLA_DV) for h in range(GLA_HEADS)]
        o_intra = [_dot(att, v[:, vs]) for att, vs in zip(atts, vss)]
        yield
        s_prev = st_s[d]
        s_prev_b = s_prev.astype(BF16)
        s_new = jnp.exp(g_last) * s_prev
        for h, vs in enumerate(vss):
            o_h = o_intra[h] + _dot(qhs[h], s_prev_b, NT)
            if first:
                oacc_s[rs, vs] = o_h
            else:
                o_ref[rs, vs] = (oacc_s[rs, vs] + o_h).astype(o_ref.dtype)
            s_new = s_new + jnp.where(lane_s // GLA_DK == h, u_t[vs, :], 0.0)
        st_s[d] = s_new

    st_s[...] = jnp.zeros(st_s.shape, F32)
    assert nb % 2 == 0
    half = nb // 2

    unroll = 2 if half % 2 == 0 else 1

    def make(first, base):
        def body(i, carry):
            chains = []
            for u in range(unroll):
                b = base + i * unroll + u
                for j in range(2):
                    chains += [chunk_step(b, j, 0, first), chunk_step(nb - 1 - b, 1 - j, 1, first)]
            _interleave(chains)
            return carry
        return body

    lax.fori_loop(0, half // unroll, make(True, 0), 0)
    lax.fori_loop(0, half // unroll, make(False, half), 0)


def _gla(u3, w_gate_up, b_gate):
    B, L, _ = u3.shape
    wg = jnp.zeros((LANE, 2 * GLA_QK), F32)
    for d in range(2):
        r0 = SMALL_GLR + d * GLA_GATE_RANK
        wg = wg.at[r0:r0 + GLA_GATE_RANK, d * GLA_QK:(d + 1) * GLA_QK].set(w_gate_up[d])
    wg_hi, wg_lo = _split(wg)
    wg2 = jnp.concatenate([wg_hi, wg_lo], axis=0)
    bg = b_gate.reshape(1, 2 * GLA_QK)
    l_idx = np.arange(GLA_CHUNK)
    tri = np.stack([l_idx[None, :] <= l_idx[:, None], l_idx[None, :] >= l_idx[:, None]]).astype(np.float32)
    const2 = lambda b: (0, 0)
    return pl.pallas_call(
        _gla_kernel,
        grid=(B,),
        in_specs=[
            pl.BlockSpec((None, L, GLA_QK), lambda b: (b, 0, COL_QC // GLA_QK)),
            pl.BlockSpec((None, L, GLA_QK), lambda b: (b, 0, COL_KC // GLA_QK)),
            pl.BlockSpec((None, L, GLA_WIDTH), lambda b: (b, 0, COL_VC // GLA_WIDTH)),
            pl.BlockSpec((None, L, LANE), lambda b: (b, 0, COL_SMALL // LANE)),
            pl.BlockSpec((2 * LANE, 2 * GLA_QK), const2),
            pl.BlockSpec((1, 2 * GLA_QK), const2),
            pl.BlockSpec((2, GLA_CHUNK, GLA_CHUNK), lambda b: (0, 0, 0)),
        ],
        out_specs=pl.BlockSpec((None, L, GLA_WIDTH), lambda b: (b, 0, 0)),
        out_shape=jax.ShapeDtypeStruct((B, L, GLA_WIDTH), BF16),
        scratch_shapes=[
            pltpu.VMEM((L, 2 * GLA_QK), F32),
            pltpu.VMEM((L, GLA_WIDTH), F32),
            pltpu.VMEM((L // GLA_BLK, GLA_WIDTH, GLA_BLK), BF16),
            pltpu.VMEM((2, GLA_DV, GLA_QK), F32),
        ],
        compiler_params=_params("parallel"),
        name="gla",
    )(u3, u3, u3, u3, wg2, bg, jnp.asarray(tri, BF16))


def _merge_kernel(gm_ref, za_ref, zb_ref, zc_ref, zd_ref, ya_ref, ob_ref, oc_ref, od_ref, x_ref,
                  wa_ref, wb_ref, wc_ref, wd_ref, wo_ref, ga_ref, gc_ref, fg_ref, o_ref, *, final_norm):
    def silu_b(z):
        hz = z * 0.5
        return hz + hz * jnp.tanh(hz)

    def gated(o_ref_, z_ref_):
        return o_ref_[...] * silu_b(z_ref_[...])

    def gate_mix(k, y):
        g = gm_ref[:, k * D_MODEL:(k + 1) * D_MODEL]
        return (0.5 + 0.5 * jnp.tanh(g * 0.5)) * y.astype(BF16)

    ya = gated(ya_ref, za_ref).astype(F32)
    ms = jnp.mean(ya * ya, axis=-1, keepdims=True)
    ya = (ya * lax.rsqrt(ms + NORM_EPS) * ga_ref[...]).astype(BF16)
    mixed = gate_mix(0, _dot(ya, wa_ref[...]))

    yb = gated(ob_ref, zb_ref)
    mixed = mixed + gate_mix(1, _dot(yb, wb_ref[...]))

    oc = oc_ref[...].astype(F32)
    parts = []
    for h in range(GLA_HEADS):
        oh = oc[:, h * GLA_DV:(h + 1) * GLA_DV]
        ms = jnp.mean(oh * oh, axis=-1, keepdims=True)
        parts.append(oh * lax.rsqrt(ms + NORM_EPS))
    yc = (jnp.concatenate(parts, axis=1) * gc_ref[...]).astype(BF16) * silu_b(zc_ref[...])
    mixed = mixed + gate_mix(2, _dot(yc, wc_ref[...]))

    yd = gated(od_ref, zd_ref)
    mixed = mixed + gate_mix(3, _dot(yd, wd_ref[...]))

    x = x_ref[...] + _dot(mixed, wo_ref[...])
    if final_norm:
        ms = jnp.mean(x * x, axis=-1, keepdims=True)
        x = x * lax.rsqrt(ms + NORM_EPS) * fg_ref[...]
    o_ref[...] = x


def _merge(u2, ya, ob, oc, od, x2, wa, wb, wc, wd, wo, ga, gc, fg, layer, final_norm):
    t = x2.shape[0]
    tm = min(512, t)
    row = lambda c: (lambda i: (i, c))
    const2 = lambda i: (0, 0)
    of_layer = lambda i: (layer, 0, 0)
    resident = pl.Buffered(1)
    return pl.pallas_call(
        functools.partial(_merge_kernel, final_norm=final_norm),
        grid=(t // tm,),
        in_specs=[
            pl.BlockSpec((tm, N_BRANCH * D_MODEL), row(COL_GMERGE // (N_BRANCH * D_MODEL))),
            pl.BlockSpec((tm, SSD_INNER), row(COL_ZA // SSD_INNER)),
            pl.BlockSpec((tm, MLA_WIDTH), row(COL_ZB // MLA_WIDTH)),
            pl.BlockSpec((tm, GLA_WIDTH), row(COL_ZC // GLA_WIDTH)),
            pl.BlockSpec((tm, GQA_WIDTH), row(COL_ZD // GQA_WIDTH)),
            pl.BlockSpec((tm, SSD_INNER), row(0)),
            pl.BlockSpec((tm, MLA_WIDTH), row(0)),
            pl.BlockSpec((tm, GLA_WIDTH), row(0)),
            pl.BlockSpec((tm, GQA_WIDTH), row(0)),
            pl.BlockSpec((tm, D_MODEL), row(0)),
            pl.BlockSpec((None, SSD_INNER, D_MODEL), of_layer, pipeline_mode=resident),
            pl.BlockSpec((None, MLA_WIDTH, D_MODEL), of_layer, pipeline_mode=resident),
            pl.BlockSpec((None, GLA_WIDTH, D_MODEL), of_layer, pipeline_mode=resident),
            pl.BlockSpec((None, GQA_WIDTH, D_MODEL), of_layer, pipeline_mode=resident),
            pl.BlockSpec((None, D_MODEL, D_MODEL), of_layer, pipeline_mode=resident),
            pl.BlockSpec((None, 1, SSD_INNER), of_layer),
            pl.BlockSpec((None, 1, GLA_WIDTH), of_layer),
            pl.BlockSpec((1, D_MODEL), const2),
        ],
        out_specs=pl.BlockSpec((tm, D_MODEL), row(0)),
        out_shape=jax.ShapeDtypeStruct((t, D_MODEL), F32),
        compiler_params=_params("parallel"),
        name="merge",
    )(u2, u2, u2, u2, u2, ya, ob, oc, od, x2, wa, wb, wc, wd, wo, ga, gc, fg)


def _reorder_w_in(w):
    w = w.astype(BF16)
    split_idx = [int(v) for v in np.cumsum(IN_WIDTHS)[:-1]]
    (g_merge, z_a, xbc, dt_raw, z_b, q_lat, kv_lat, k_rope, z_c, q_c, k_c, v_c, g_lr,
     z_d, q_d, k_d, v_d) = jnp.split(w, split_idx, axis=-1)
    xs, bm, cm = xbc[..., :SSD_INNER], xbc[..., SSD_INNER:SSD_INNER + SSD_BC], xbc[..., SSD_INNER + SSD_BC:]
    small = jnp.concatenate([dt_raw, g_lr, k_rope, jnp.zeros(w.shape[:-1] + (LANE - 96,), w.dtype)], axis=-1)
    cols = [g_merge, z_a, xs, z_b, z_c, z_d, q_d, v_c, q_lat, small, bm, cm, kv_lat, q_c, k_c, k_d, v_d]
    out = jnp.concatenate(cols, axis=-1)
    assert out.shape[-1] == N_PROJ
    return out


def kernel(x, norm_g, w_in, conv_w, conv_b, a_log, dt_bias, d_skip, ssd_norm_g, q_lat_norm_g, kv_lat_norm_g, w_q_b, w_kv_b, w_gate_up, b_gate, gla_norm_g, q_norm_g, k_norm_g, w_br_a, w_br_b, w_br_c, w_br_d, w_out, final_g):
    B, L, D = x.shape
    depth = w_in.shape[0]
    x2 = x.reshape(B * L, D)
    w_in_b = _reorder_w_in(w_in)
    norm_g3 = norm_g[:, None, :]
    wa, wb, wc, wd, wo = (w.astype(BF16) for w in (w_br_a, w_br_b, w_br_c, w_br_d, w_out))
    ga3, gc3 = ssd_norm_g[:, None, :], gla_norm_g[:, None, :]
    for i in range(depth):
        u2 = _inproj(x2, norm_g3, w_in_b, i)
        u3 = u2.reshape(B, L, N_PROJ)
        ya = _ssd(u3, conv_w[i], conv_b[i], a_log[i], dt_bias[i], d_skip[i])
        ob = _mla(u3, q_lat_norm_g[i], kv_lat_norm_g[i], w_q_b[i], w_kv_b[i])
        oc = _gla(u3, w_gate_up[i], b_gate[i])
        od = _gqa(u3, q_norm_g[i], k_norm_g[i])
        x2 = _merge(
            u2, ya.reshape(B * L, -1), ob.reshape(B * L, -1), oc.reshape(B * L, -1), od.reshape(B * L, -1), x2,
            wa, wb, wc, wd, wo, ga3, gc3, final_g[None, :], layer=i, final_norm=(i == depth - 1))
    return x2.reshape(B, L, D)
```

```python
import functools

import numpy as np
import jax
import jax.numpy as jnp
from jax import lax
from jax.experimental import pallas as pl
from jax.experimental.pallas import tpu as pltpu

F32 = jnp.float32
BF16 = jnp.bfloat16

D_MODEL = 1024
GRID_W = 64
ROPE_THETA = 10000.0
NORM_EPS = 1e-6
N_BRANCH = 4

SSD_HEADS = 16
SSD_HEAD_DIM = 64
SSD_INNER = SSD_HEADS * SSD_HEAD_DIM
SSD_GROUPS = 2
SSD_STATE = 128
SSD_CONV = 5
SSD_CHUNK = 128
SSD_BC = SSD_GROUPS * SSD_STATE

MLA_HEADS = 8
MLA_Q_RANK = 384
MLA_KV_RANK = 256
MLA_NOPE = 64
MLA_ROPE = 32
MLA_V = 64
MLA_WIDTH = MLA_HEADS * MLA_V

GLA_HEADS = 4
GLA_DK = 64
GLA_DV = 128
GLA_GATE_RANK = 16
GLA_TAU = 16.0
GLA_CHUNK = 64
GLA_WIDTH = GLA_HEADS * GLA_DV
GLA_QK = GLA_HEADS * GLA_DK

GQA_HEADS = 8
GQA_KV_HEADS = 2
GQA_HEAD_DIM = 64
GQA_WIDTH = GQA_HEADS * GQA_HEAD_DIM
GQA_REP = GQA_HEADS // GQA_KV_HEADS
GQA_GROUP_W = GQA_REP * GQA_HEAD_DIM

IN_WIDTHS = (
    N_BRANCH * D_MODEL, SSD_INNER, SSD_INNER + 2 * SSD_BC, 2 * SSD_HEADS, MLA_WIDTH, MLA_Q_RANK,
    MLA_KV_RANK, MLA_ROPE, GLA_WIDTH, GLA_QK, GLA_QK, GLA_WIDTH, 2 * GLA_GATE_RANK, GQA_WIDTH,
    GQA_WIDTH, GQA_KV_HEADS * GQA_HEAD_DIM, GQA_KV_HEADS * GQA_HEAD_DIM,
)

LANE = 128

SMALL_DT = 0
SMALL_GLR = 32
SMALL_KROPE = 64
COL_GMERGE = 0
COL_ZA = 4096
COL_XS = 5120
COL_ZB = 6144
COL_ZC = 6656
COL_ZD = 7168
COL_QD = 7680
COL_VC = 8192
COL_QLAT = 8704
COL_SMALL = COL_QLAT + MLA_Q_RANK
COL_BM = 9216
COL_CM = 9472
COL_KVLAT = 9728
COL_QC = 9984
COL_KC = 10240
COL_KD = 10496
N_PROJ = 10752

VMEM_LIMIT = 56 * 1024 * 1024

NN = ((1,), (0,))
NT = ((1,), (1,))
TN = ((0,), (0,))


def _dot(a, b, dims=NN):
    return lax.dot_general(a, b, (dims, ((), ())), preferred_element_type=F32)


def _split(a):
    hi = a.astype(BF16)
    lo = (a - hi.astype(F32)).astype(BF16)
    return hi, lo


def _softplus(x):
    return jnp.maximum(x, 0.0) + jnp.log1p(jnp.exp(-jnp.abs(x)))


def _silu(x):
    return x * (1.0 / (1.0 + jnp.exp(-x)))


def _params(*sem):
    return pltpu.CompilerParams(dimension_semantics=sem, vmem_limit_bytes=VMEM_LIMIT)


def _interleave(gens):
    gens = list(gens)
    while gens:
        alive = []
        for g in gens:
            try:
                next(g)
                alive.append(g)
            except StopIteration:
                pass
        gens = alive


def _both_directions(nc, chunk_step, unroll=1):
    assert nc % 2 == 0
    half = nc // 2
    if half % unroll:
        unroll = 1

    def make(first, base):
        def body(i, carry):
            chains = []
            for u in range(unroll):
                c = base + i * unroll + u
                chains += [chunk_step(c, 0, first), chunk_step(nc - 1 - c, 1, first)]
            _interleave(chains)
            return carry
        return body

    lax.fori_loop(0, half // unroll, make(True, 0), 0)
    lax.fori_loop(0, half // unroll, make(False, half), 0)


def _inproj_kernel(x_ref, g_ref, w_ref, o_ref, h_ref):
    @pl.when(pl.program_id(1) == 0)
    def _():
        x = x_ref[...]
        ms = jnp.mean(x * x, axis=-1, keepdims=True)
        h_ref[...] = (x * lax.rsqrt(ms + NORM_EPS) * g_ref[...]).astype(BF16)

    o_ref[...] = _dot(h_ref[...], w_ref[...]).astype(BF16)


def _inproj(x2, g, w, layer):
    t = x2.shape[0]
    tm = min(2048, t)
    tn = 1536
    return pl.pallas_call(
        _inproj_kernel,
        grid=(t // tm, N_PROJ // tn),
        in_specs=[
            pl.BlockSpec((tm, D_MODEL), lambda i, j: (i, 0)),
            pl.BlockSpec((None, 1, D_MODEL), lambda i, j: (layer, 0, 0)),
            pl.BlockSpec((None, D_MODEL, tn), lambda i, j: (layer, 0, j)),
        ],
        out_specs=pl.BlockSpec((tm, tn), lambda i, j: (i, j)),
        out_shape=jax.ShapeDtypeStruct((t, N_PROJ), BF16),
        scratch_shapes=[pltpu.VMEM((tm, D_MODEL), BF16)],
        compiler_params=_params("parallel", "arbitrary"),
        name="inproj",
    )(x2, g, w)


CONV_W = 256
CONV_HALO = 8


def _ssd_kernel(xs_ref, bm_ref, cm_ref, sm_ref, cwx_ref, cwb_ref, cwc_ref, cbx_ref, cbb_ref, cbc_ref,
                dtb_ref, aneg_ref, dskip_ref, tri_ref, exp_ref, o_ref,
                pad_s, xs_s, bm_s, cm_s, dt_s, yacc_s, st_s):
    L = xs_ref.shape[0]
    Q = SSD_CHUNK
    nc = L // Q
    ext = Q + 2 * CONV_HALO

    pad_s[0:CONV_HALO, :] = jnp.zeros((CONV_HALO, CONV_W), F32)
    pad_s[L + CONV_HALO:L + 2 * CONV_HALO, :] = jnp.zeros((CONV_HALO, CONV_W), F32)

    def conv_group(src_ref, w_ref, b_ref, col, dst_ref, dst_dtype):
        pad_s[CONV_HALO:L + CONV_HALO, :] = src_ref[:, col:col + CONV_W].astype(F32)
        w = w_ref[:, col:col + CONV_W]
        b = b_ref[:, col:col + CONV_W]

        def body(r, carry):
            base = pl.multiple_of(r * Q, Q)
            win = pad_s[pl.ds(base, ext), :]
            acc = jnp.zeros((Q, CONV_W), F32) + b
            for k in range(SSD_CONV):
                shift = (SSD_CONV // 2 - k) % ext
                sh = win if shift == 0 else pltpu.roll(win, shift, 0)
                acc = acc + sh[CONV_HALO:CONV_HALO + Q, :] * w[k:k + 1, :]
            dst_ref[pl.ds(base, Q), col:col + CONV_W] = _silu(acc).astype(dst_dtype)
            return carry

        lax.fori_loop(0, nc, body, 0)

    for cg in range(SSD_INNER // CONV_W):
        conv_group(xs_ref, cwx_ref, cbx_ref, cg * CONV_W, xs_s, F32)
    conv_group(bm_ref, cwb_ref, cbb_ref, 0, bm_s, BF16)
    conv_group(cm_ref, cwc_ref, cbc_ref, 0, cm_s, BF16)

    dt_s[...] = _softplus(sm_ref[...].astype(F32) + dtb_ref[...])

    li = lax.broadcasted_iota(jnp.int32, (Q, Q), 0)
    si = lax.broadcasted_iota(jnp.int32, (Q, Q), 1)
    lane = lax.broadcasted_iota(jnp.int32, (Q, SSD_INNER), 1)
    low_half = (lane % LANE) < SSD_HEAD_DIM
    gw = SSD_INNER // SSD_GROUPS
    heads_per_group = SSD_HEADS // SSD_GROUPS

    def chunk_step(c, d, first):
        rs = pl.ds(pl.multiple_of(c * Q, Q), Q)
        keep = (si <= li) if d == 0 else (si >= li)
        tri = tri_ref[d]
        dt_c = dt_s[rs, :]
        da_c = dt_c * aneg_ref[...]
        da_hi, da_lo = _split(da_c)
        ac = _dot(tri, da_hi) + _dot(tri, da_lo)
        tri_t = tri_ref[1 - d]
        ac_t = _dot(da_hi, tri_t, TN) + _dot(da_lo, tri_t, TN)
        yield
        ac_hi, ac_lo = _split(ac)
        ac_x = _dot(jnp.concatenate([ac_hi, ac_lo], axis=1), exp_ref[d])
        dt_hi, dt_lo = _split(dt_c)
        dt_x = _dot(jnp.concatenate([dt_hi, dt_lo], axis=1), exp_ref[d])
        cm_c = cm_s[rs, :]
        bm_c = bm_s[rs, :]
        cmgs = [cm_c[:, g * SSD_STATE:(g + 1) * SSD_STATE] for g in range(SSD_GROUPS)]
        bmgs = [bm_c[:, g * SSD_STATE:(g + 1) * SSD_STATE] for g in range(SSD_GROUPS)]
        cbs = [_dot(cmg, bmg, NT) for cmg, bmg in zip(cmgs, bmgs)]
        yield
        xs_c = xs_s[rs, :]
        xd = xs_c * dt_x
        xd_b = xd.astype(BF16)
        a_last = ac_x[Q - 1:Q, :] if d == 0 else ac_x[0:1, :]
        xdd = (xd * jnp.exp(a_last - ac_x)).astype(BF16)
        eac = jnp.exp(ac_x)
        xd_lo = jnp.where(low_half, xd_b, jnp.zeros_like(xd_b))
        xd_hi = jnp.where(low_half, jnp.zeros_like(xd_b), xd_b)
        st_news = [_dot(bmgs[g], xdd[:, g * gw:(g + 1) * gw], TN) for g in range(SSD_GROUPS)]
        yield
        y_diag = {}
        for g in range(SSD_GROUPS):
            for pair in range(heads_per_group // 2):
                c0 = g * gw + pair * LANE
                lms = []
                for q in range(2):
                    e = d * SSD_HEADS + g * heads_per_group + 2 * pair + q
                    seg = ac[:, e:e + 1] - ac_t[e:e + 1, :]
                    lms.append((jnp.where(keep, jnp.exp(seg), 0.0) * cbs[g]).astype(BF16))
                x_pair = jnp.concatenate([xd_lo[:, c0:c0 + LANE], xd_hi[:, c0:c0 + LANE]], axis=0)
                y = _dot(jnp.concatenate(lms, axis=1), x_pair)
                if d == 0:
                    y = y + xs_c[:, c0:c0 + LANE] * dskip_ref[:, c0:c0 + LANE]
                y_diag[c0] = y
            yield
        for g in range(SSD_GROUPS):
            s_prev = st_s[d, g]
            y_off = _dot(cmgs[g], s_prev.astype(BF16)) * eac[:, g * gw:(g + 1) * gw]
            st_s[d, g] = jnp.exp(a_last[:, g * gw:(g + 1) * gw]) * s_prev + st_news[g]
            for pair in range(heads_per_group // 2):
                c0 = g * gw + pair * LANE
                y_pair = y_diag[c0] + y_off[:, pair * LANE:(pair + 1) * LANE]
                if first:
                    yacc_s[rs, c0:c0 + LANE] = y_pair
                else:
                    o_ref[rs, c0:c0 + LANE] = (yacc_s[rs, c0:c0 + LANE] + y_pair).astype(o_ref.dtype)

    st_s[...] = jnp.zeros(st_s.shape, F32)
    _both_directions(nc, chunk_step, unroll=2)


def _ssd(u3, conv_w, conv_b, a_log, dt_bias, d_skip):
    B, L, _ = u3.shape
    cwx, cwb, cwc = conv_w[:, :SSD_INNER], conv_w[:, SSD_INNER:SSD_INNER + SSD_BC], conv_w[:, SSD_INNER + SSD_BC:]
    cb2 = conv_b[None, :]
    cbx, cbb, cbc = cb2[:, :SSD_INNER], cb2[:, SSD_INNER:SSD_INNER + SSD_BC], cb2[:, SSD_INNER + SSD_BC:]
    zpad = jnp.zeros((LANE - 2 * SSD_HEADS,), F32)
    dtb = jnp.concatenate([dt_bias.reshape(-1), zpad])[None, :]
    aneg = jnp.concatenate([-jnp.exp(a_log.reshape(-1)), zpad])[None, :]
    dskip = jnp.repeat(d_skip, SSD_HEAD_DIM)[None, :]
    l_idx = np.arange(SSD_CHUNK)
    tri = np.stack([l_idx[None, :] <= l_idx[:, None], l_idx[None, :] >= l_idx[:, None]]).astype(np.float32)
    expm = np.zeros((2, 2 * LANE, SSD_INNER), np.float32)
    for d in range(2):
        for e in range(SSD_HEADS):
            for part in range(2):
                expm[d, part * LANE + d * SSD_HEADS + e, e * SSD_HEAD_DIM:(e + 1) * SSD_HEAD_DIM] = 1.0
    const2 = lambda b: (0, 0)
    const3 = lambda b: (0, 0, 0)
    return pl.pallas_call(
        _ssd_kernel,
        grid=(B,),
        in_specs=[
            pl.BlockSpec((None, L, SSD_INNER), lambda b: (b, 0, COL_XS // SSD_INNER)),
            pl.BlockSpec((None, L, SSD_BC), lambda b: (b, 0, COL_BM // SSD_BC)),
            pl.BlockSpec((None, L, SSD_BC), lambda b: (b, 0, COL_CM // SSD_BC)),
            pl.BlockSpec((None, L, LANE), lambda b: (b, 0, COL_SMALL // LANE)),
            pl.BlockSpec((SSD_CONV, SSD_INNER), const2),
            pl.BlockSpec((SSD_CONV, SSD_BC), const2),
            pl.BlockSpec((SSD_CONV, SSD_BC), const2),
            pl.BlockSpec((1, SSD_INNER), const2),
            pl.BlockSpec((1, SSD_BC), const2),
            pl.BlockSpec((1, SSD_BC), const2),
            pl.BlockSpec((1, LANE), const2),
            pl.BlockSpec((1, LANE), const2),
            pl.BlockSpec((1, SSD_INNER), const2),
            pl.BlockSpec((2, SSD_CHUNK, SSD_CHUNK), const3),
            pl.BlockSpec((2, 2 * LANE, SSD_INNER), const3),
        ],
        out_specs=pl.BlockSpec((None, L, SSD_INNER), lambda b: (b, 0, 0)),
        out_shape=jax.ShapeDtypeStruct((B, L, SSD_INNER), BF16),
        scratch_shapes=[
            pltpu.VMEM((L + 2 * CONV_HALO, CONV_W), F32),
            pltpu.VMEM((L, SSD_INNER), F32),
            pltpu.VMEM((L, SSD_BC), BF16),
            pltpu.VMEM((L, SSD_BC), BF16),
            pltpu.VMEM((L, LANE), F32),
            pltpu.VMEM((L, SSD_INNER), F32),
            pltpu.VMEM((2, SSD_GROUPS, SSD_STATE, SSD_INNER // SSD_GROUPS), F32),
        ],
        compiler_params=_params("parallel"),
        name="ssd",
    )(u3, u3, u3, u3, cwx, cwb, cwc, cbx, cbb, cbc, dtb, aneg, dskip,
      jnp.asarray(tri, BF16), jnp.asarray(expm, BF16))


LOG2E = 1.4426950408889634


def _rope(x, cos, sin_a, sin_b, half):
    w = x.shape[-1]
    return x * cos + pltpu.roll(x, w - half, 1) * sin_a + pltpu.roll(x, half, 1) * sin_b


ATTN_TQ = 256
ATTN_CHAINS = 16


def _attn_tq(L):
    return ATTN_TQ if L % ATTN_TQ == 0 else L


def _attend_chain(load_q, load_k, load_v, finish):
    s = _dot(load_q(), load_k(), NT)
    yield
    m = jnp.max(s, axis=-1, keepdims=True)
    p = jnp.exp2(s - m)
    r = 1.0 / jnp.sum(p, axis=-1, keepdims=True)
    pb = p.astype(BF16)
    yield
    finish(_dot(pb, load_v()) * r)


def _skewed(gens):
    pending = list(gens)
    active = []
    while pending or active:
        if pending:
            active.append(pending.pop(0))
        alive = []
        for g in active:
            try:
                next(g)
                alive.append(g)
            except StopIteration:
                pass
        active = alive


def _attn_loop(L, heads, load_q, load_k, load_v, store):
    tq = _attn_tq(L)
    n = L // tq
    tiles = ATTN_CHAINS // heads
    if n % tiles:
        tiles = 1

    def body(i, carry):
        chains = []
        for u in range(tiles):
            rs = pl.ds(pl.multiple_of((i * tiles + u) * tq, tq), tq)
            acc = []
            for h in range(heads):
                def finish(o, acc=acc, rs=rs, last=(h == heads - 1)):
                    acc.append(o)
                    if last:
                        store(rs, functools.reduce(lambda a, b: a + b, acc))
                chains.append(_attend_chain(functools.partial(load_q, h, rs), functools.partial(load_k, h),
                                            functools.partial(load_v, h), finish))
        _skewed(chains)
        return carry

    lax.fori_loop(0, n // tiles, body, 0)


MLA_PAIR = 2


def _mla_kernel(ql_ref, kv_ref, gq_ref, gkv_ref, wq_ref, wk_ref, wv_ref, cosq_ref, cosk_ref, sa_ref, sb_ref,
                o_ref, qn_s, kvn_s, kr_s, q_s, k_s, v_s):
    L = ql_ref.shape[0]
    tq = _attn_tq(L)
    half = MLA_ROPE // 4

    @pl.when(pl.program_id(1) == 0)
    def _():
        ql = ql_ref[:, 0:MLA_Q_RANK].astype(F32)
        ms = jnp.mean(ql * ql, axis=-1, keepdims=True)
        qn_s[...] = (ql * lax.rsqrt(ms + NORM_EPS) * gq_ref[...]).astype(BF16)
        kv = kv_ref[...].astype(F32)
        ms = jnp.mean(kv * kv, axis=-1, keepdims=True)
        kvn_s[...] = (kv * lax.rsqrt(ms + NORM_EPS) * gkv_ref[...]).astype(BF16)
        sm = ql_ref[:, MLA_Q_RANK:MLA_Q_RANK + LANE].astype(F32)
        kr_s[...] = _rope(sm, cosk_ref[...], sa_ref[...], sb_ref[...], half)

    scale = (MLA_NOPE + MLA_ROPE) ** -0.5 * LOG2E
    q2 = _dot(qn_s[...], wq_ref[...])
    k2 = _dot(kvn_s[...], wk_ref[...])
    v2 = _dot(kvn_s[...], wv_ref[...]).astype(BF16)
    lane_v = lax.broadcasted_iota(jnp.int32, (L, LANE), 1)
    for h in range(MLA_PAIR):
        hs = slice(h * LANE, (h + 1) * LANE)
        q = _rope(q2[:, hs], cosq_ref[...], sa_ref[...], sb_ref[...], half) * scale
        q_s[h] = q.astype(BF16)
        k_s[h] = (k2[:, hs] + kr_s[...]).astype(BF16)
        v_s[h] = jnp.where(lane_v // MLA_V == h, v2, jnp.zeros_like(v2))

    def store(rs, o):
        o_ref[rs, :] = o.astype(o_ref.dtype)

    _attn_loop(L, MLA_PAIR, lambda h, rs: q_s[h, rs, :], lambda h: k_s[h], lambda h: v_s[h], store)


def _rope_tables(L, d_rot, width, lane0):
    rows = L // GRID_W
    row = jnp.repeat(jnp.arange(rows), GRID_W).astype(F32)
    col = jnp.tile(jnp.arange(GRID_W), rows).astype(F32)
    m = d_rot // 2
    inv = ROPE_THETA ** (-jnp.arange(0, m, 2, dtype=F32) / m)
    ang_r = row[:, None] * inv
    ang_c = col[:, None] * inv
    ang = jnp.concatenate([ang_r, ang_r, ang_c, ang_c], axis=-1)
    cos, sin = jnp.cos(ang), jnp.sin(ang)
    first = (jnp.arange(d_rot) % m) < (m // 2)
    sin_a = jnp.where(first, -sin, 0.0)
    sin_b = jnp.where(first, 0.0, sin)
    if lane0 is None:
        reps = width // d_rot
        return tuple(jnp.tile(t, (1, reps)) for t in (cos, sin_a, sin_b))
    padl, padr = lane0, width - lane0 - d_rot
    return tuple(jnp.pad(t, ((0, 0), (padl, padr))) for t in (cos, sin_a, sin_b))


def _mla(u3, gq, gkv, w_q_b, w_kv_b):
    B, L, _ = u3.shape
    npair = MLA_HEADS // MLA_PAIR
    dq = MLA_NOPE + MLA_ROPE
    wq = w_q_b.reshape(MLA_Q_RANK, MLA_HEADS, dq).transpose(1, 0, 2)
    wq = jnp.pad(wq, ((0, 0), (0, 0), (0, LANE - dq))).astype(BF16)
    wkv = w_kv_b.reshape(MLA_KV_RANK, MLA_HEADS, MLA_NOPE + MLA_V).transpose(1, 0, 2)
    wk = jnp.pad(wkv[..., :MLA_NOPE], ((0, 0), (0, 0), (0, LANE - MLA_NOPE))).astype(BF16)
    wv = wkv[..., MLA_NOPE:].astype(BF16)

    def by_pair(w):
        h, k, n = w.shape
        return w.reshape(npair, MLA_PAIR, k, n).transpose(0, 2, 1, 3).reshape(npair, k, MLA_PAIR * n)

    wq, wk, wv = by_pair(wq), by_pair(wk), by_pair(wv)
    cos, sin_a, sin_b = _rope_tables(L, MLA_ROPE, LANE, SMALL_KROPE)
    cosq = cos.at[:, :MLA_NOPE].set(1.0)
    const2 = lambda b, p: (0, 0)
    return pl.pallas_call(
        _mla_kernel,
        grid=(B, npair),
        in_specs=[
            pl.BlockSpec((None, L, 512), lambda b, p: (b, 0, COL_QLAT // 512)),
            pl.BlockSpec((None, L, MLA_KV_RANK), lambda b, p: (b, 0, COL_KVLAT // MLA_KV_RANK)),
            pl.BlockSpec((1, MLA_Q_RANK), const2),
            pl.BlockSpec((1, MLA_KV_RANK), const2),
            pl.BlockSpec((None, MLA_Q_RANK, MLA_PAIR * LANE), lambda b, p: (p, 0, 0)),
            pl.BlockSpec((None, MLA_KV_RANK, MLA_PAIR * LANE), lambda b, p: (p, 0, 0)),
            pl.BlockSpec((None, MLA_KV_RANK, MLA_PAIR * MLA_V), lambda b, p: (p, 0, 0)),
            pl.BlockSpec((L, LANE), const2),
            pl.BlockSpec((L, LANE), const2),
            pl.BlockSpec((L, LANE), const2),
            pl.BlockSpec((L, LANE), const2),
        ],
        out_specs=pl.BlockSpec((None, L, LANE), lambda b, p: (b, 0, p)),
        out_shape=jax.ShapeDtypeStruct((B, L, MLA_WIDTH), BF16),
        scratch_shapes=[
            pltpu.VMEM((L, MLA_Q_RANK), BF16),
            pltpu.VMEM((L, MLA_KV_RANK), BF16),
            pltpu.VMEM((L, LANE), F32),
            pltpu.VMEM((MLA_PAIR, L, LANE), BF16),
            pltpu.VMEM((MLA_PAIR, L, LANE), BF16),
            pltpu.VMEM((MLA_PAIR, L, LANE), BF16),
        ],
        compiler_params=_params("parallel", "arbitrary"),
        name="mla",
    )(u3, u3, gq[None, :], gkv[None, :], wq, wk, wv, cosq, cos, sin_a, sin_b)


def _gqa_kernel(q_ref, kv_ref, gq_ref, gk_ref, bd_ref, rep_ref, cos_ref, sa_ref, sb_ref, o_ref,
                q_s, k_s, v_s):
    L = q_ref.shape[0]
    tq = _attn_tq(L)
    half = GQA_HEAD_DIM // 4
    kw = GQA_KV_HEADS * GQA_HEAD_DIM

    def headnorm(x, bd, gain):
        ms = _dot((x * x).astype(BF16), bd) * (1.0 / GQA_HEAD_DIM)
        return x * lax.rsqrt(ms + NORM_EPS) * gain

    q = headnorm(q_ref[...].astype(F32), bd_ref[...], gq_ref[...])
    q = _rope(q, cos_ref[...], sa_ref[...], sb_ref[...], half)
    q_s[...] = q.astype(BF16)
    k = headnorm(kv_ref[:, 0:kw].astype(F32), bd_ref[0:kw, 0:kw], gk_ref[...])
    k = _rope(k, cos_ref[:, 0:kw], sa_ref[:, 0:kw], sb_ref[:, 0:kw], half)
    k_s[...] = _dot(k.astype(BF16), rep_ref[...]).astype(BF16)
    v_rep = _dot(kv_ref[:, kw:2 * kw], rep_ref[...]).astype(BF16)
    lane = lax.broadcasted_iota(jnp.int32, (L, GQA_GROUP_W), 1)
    for h in range(GQA_REP):
        v_s[h] = jnp.where(lane // GQA_HEAD_DIM == h, v_rep, jnp.zeros_like(v_rep))

    lane_q = lax.broadcasted_iota(jnp.int32, (tq, GQA_GROUP_W), 1)

    def load_q(h, rs):
        qb = q_s[rs, :]
        return jnp.where(lane_q // GQA_HEAD_DIM == h, qb, jnp.zeros_like(qb))

    def store(rs, o):
        o_ref[rs, :] = o.astype(o_ref.dtype)

    _attn_loop(L, GQA_REP, load_q, lambda h: k_s[...], lambda h: v_s[h], store)


def _gqa(u3, q_norm_g, k_norm_g):
    B, L, _ = u3.shape
    kw = GQA_KV_HEADS * GQA_HEAD_DIM
    gq = jnp.tile(q_norm_g, GQA_REP)[None, :] * (GQA_HEAD_DIM ** -0.5 * LOG2E)
    gk = jnp.tile(k_norm_g, GQA_KV_HEADS)[None, :]
    idx = np.arange(GQA_GROUP_W)
    bd = (idx[:, None] // GQA_HEAD_DIM == idx[None, :] // GQA_HEAD_DIM).astype(np.float32)
    rep = np.zeros((GQA_KV_HEADS, kw, GQA_GROUP_W), np.float32)
    for g in range(GQA_KV_HEADS):
        rep[g, g * GQA_HEAD_DIM + idx % GQA_HEAD_DIM, idx] = 1.0
    cos, sin_a, sin_b = _rope_tables(L, GQA_HEAD_DIM, GQA_GROUP_W, None)
    const2 = lambda b, g: (0, 0)
    return pl.pallas_call(
        _gqa_kernel,
        grid=(B, GQA_KV_HEADS),
        in_specs=[
            pl.BlockSpec((None, L, GQA_GROUP_W), lambda b, g: (b, 0, COL_QD // GQA_GROUP_W + g)),
            pl.BlockSpec((None, L, 2 * kw), lambda b, g: (b, 0, COL_KD // (2 * kw))),
            pl.BlockSpec((1, GQA_GROUP_W), const2),
            pl.BlockSpec((1, kw), const2),
            pl.BlockSpec((GQA_GROUP_W, GQA_GROUP_W), const2),
            pl.BlockSpec((None, kw, GQA_GROUP_W), lambda b, g: (g, 0, 0)),
            pl.BlockSpec((L, GQA_GROUP_W), const2),
            pl.BlockSpec((L, GQA_GROUP_W), const2),
            pl.BlockSpec((L, GQA_GROUP_W), const2),
        ],
        out_specs=pl.BlockSpec((None, L, GQA_GROUP_W), lambda b, g: (b, 0, g)),
        out_shape=jax.ShapeDtypeStruct((B, L, GQA_WIDTH), BF16),
        scratch_shapes=[
            pltpu.VMEM((L, GQA_GROUP_W), BF16),
            pltpu.VMEM((L, GQA_GROUP_W), BF16),
            pltpu.VMEM((GQA_REP, L, GQA_GROUP_W), BF16),
        ],
        compiler_params=_params("parallel", "parallel"),
        name="gqa",
    )(u3, u3, gq, gk, jnp.asarray(bd, BF16), jnp.asarray(rep, BF16), cos, sin_a, sin_b)


GLA_BLK = 2 * GLA_CHUNK


def _gla_kernel(q_ref, k_ref, v_ref, sm_ref, wg_ref, bg_ref, tri_ref, o_ref,
                glog_s, oacc_s, vt_s, st_s):
    L = q_ref.shape[0]
    Q = GLA_CHUNK
    nb = L // GLA_BLK

    def transpose_block(b, carry):
        rows = pl.ds(pl.multiple_of(b * GLA_BLK, GLA_BLK), GLA_BLK)
        vt_s[b] = v_ref[rows, :].astype(F32).T.astype(BF16)
        return carry

    lax.fori_loop(0, nb, transpose_block, 0, unroll=2 if nb % 2 == 0 else 1)

    sm = sm_ref[...]
    g_pre = _dot(jnp.concatenate([sm, sm], axis=1), wg_ref[...]) + bg_ref[...]
    glog_s[...] = (jnp.minimum(g_pre, 0.0) - jnp.log(1.0 + jnp.exp(-jnp.abs(g_pre)))) * (1.0 / GLA_TAU)

    li = lax.broadcasted_iota(jnp.int32, (Q, Q), 0)
    si = lax.broadcasted_iota(jnp.int32, (Q, Q), 1)
    lane = lax.broadcasted_iota(jnp.int32, (Q, GLA_QK), 1)
    lane_s = lax.broadcasted_iota(jnp.int32, (GLA_DV, GLA_QK), 1)
    k_zero = jnp.zeros((Q, GLA_QK), BF16)

    def chunk_step(blk, j, d, first):
        rs = pl.ds(pl.multiple_of(blk * GLA_BLK + j * Q, Q), Q)
        keep = (si <= li) if d == 0 else (si >= li)
        g_c = glog_s[rs, d * GLA_QK:(d + 1) * GLA_QK]
        g_hi, g_lo = _split(g_c)
        gcum = _dot(tri_ref[d], g_hi) + _dot(tri_ref[d], g_lo)
        yield
        g_last = gcum[Q - 1:Q, :] if d == 0 else gcum[0:1, :]
        q = q_ref[rs, :].astype(F32) * (GLA_DK ** -0.5)
        k = k_ref[rs, :].astype(F32)
        v = v_ref[rs, :]
        qg = (q * jnp.exp(gcum)).astype(BF16)
        kg = (k * jnp.exp(-gcum)).astype(BF16)
        k_end = (k * jnp.exp(g_last - gcum)).astype(BF16)
        k_pad = jnp.concatenate([k_end, k_zero] if j == 0 else [k_zero, k_end], axis=0)
        u_t = _dot(vt_s[blk], k_pad)
        qhs = [jnp.where(lane // GLA_DK == h, qg, jnp.zeros_like(qg)) for h in range(GLA_HEADS)]
        yield
        atts = [jnp.where(keep, _dot(qh, kg, NT), 0.0).astype(BF16) for qh in qhs]
        yield
        vss = [slice(h * GLA_DV, (h + 1) * GLA_DV) for h in range(GLA_HEADS)]
        o_intra = [_dot(att, v[:, vs]) for att, vs in zip(atts, vss)]
        yield
        s_prev = st_s[d]
        s_prev_b = s_prev.astype(BF16)
        s_new = jnp.exp(g_last) * s_prev
        for h, vs in enumerate(vss):
            o_h = o_intra[h] + _dot(qhs[h], s_prev_b, NT)
            if first:
                oacc_s[rs, vs] = o_h
            else:
                o_ref[rs, vs] = (oacc_s[rs, vs] + o_h).astype(o_ref.dtype)
            s_new = s_new + jnp.where(lane_s // GLA_DK == h, u_t[vs, :], 0.0)
        st_s[d] = s_new

    st_s[...] = jnp.zeros(st_s.shape, F32)
    assert nb % 2 == 0
    half = nb // 2

    unroll = 2 if half % 2 == 0 else 1

    def make(first, base):
        def body(i, carry):
            chains = []
            for u in range(unroll):
                b = base + i * unroll + u
                for j in range(2):
                    chains += [chunk_step(b, j, 0, first), chunk_step(nb - 1 - b, 1 - j, 1, first)]
            _interleave(chains)
            return carry
        return body

    lax.fori_loop(0, half // unroll, make(True, 0), 0)
    lax.fori_loop(0, half // unroll, make(False, half), 0)


def _gla(u3, w_gate_up, b_gate):
    B, L, _ = u3.shape
    wg = jnp.zeros((LANE, 2 * GLA_QK), F32)
    for d in range(2):
        r0 = SMALL_GLR + d * GLA_GATE_RANK
        wg = wg.at[r0:r0 + GLA_GATE_RANK, d * GLA_QK:(d + 1) * GLA_QK].set(w_gate_up[d])
    wg_hi, wg_lo = _split(wg)
    wg2 = jnp.concatenate([wg_hi, wg_lo], axis=0)
    bg = b_gate.reshape(1, 2 * GLA_QK)
    l_idx = np.arange(GLA_CHUNK)
    tri = np.stack([l_idx[None, :] <= l_idx[:, None], l_idx[None, :] >= l_idx[:, None]]).astype(np.float32)
    const2 = lambda b: (0, 0)
    return pl.pallas_call(
        _gla_kernel,
        grid=(B,),
        in_specs=[
            pl.BlockSpec((None, L, GLA_QK), lambda b: (b, 0, COL_QC // GLA_QK)),
            pl.BlockSpec((None, L, GLA_QK), lambda b: (b, 0, COL_KC // GLA_QK)),
            pl.BlockSpec((None, L, GLA_WIDTH), lambda b: (b, 0, COL_VC // GLA_WIDTH)),
            pl.BlockSpec((None, L, LANE), lambda b: (b, 0, COL_SMALL // LANE)),
            pl.BlockSpec((2 * LANE, 2 * GLA_QK), const2),
            pl.BlockSpec((1, 2 * GLA_QK), const2),
            pl.BlockSpec((2, GLA_CHUNK, GLA_CHUNK), lambda b: (0, 0, 0)),
        ],
        out_specs=pl.BlockSpec((None, L, GLA_WIDTH), lambda b: (b, 0, 0)),
        out_shape=jax.ShapeDtypeStruct((B, L, GLA_WIDTH), BF16),
        scratch_shapes=[
            pltpu.VMEM((L, 2 * GLA_QK), F32),
            pltpu.VMEM((L, GLA_WIDTH), F32),
            pltpu.VMEM((L // GLA_BLK, GLA_WIDTH, GLA_BLK), BF16),
            pltpu.VMEM((2, GLA_DV, GLA_QK), F32),
        ],
        compiler_params=_params("parallel"),
        name="gla",
    )(u3, u3, u3, u3, wg2, bg, jnp.asarray(tri, BF16))


def _merge_kernel(gm_ref, za_ref, zb_ref, zc_ref, zd_ref, ya_ref, ob_ref, oc_ref, od_ref, x_ref,
                  wa_ref, wb_ref, wc_ref, wd_ref, wo_ref, ga_ref, gc_ref, fg_ref, o_ref, *, final_norm):
    def silu_b(z):
        hz = z * 0.5
        return hz + hz * jnp.tanh(hz)

    def gated(o_ref_, z_ref_):
        return o_ref_[...] * silu_b(z_ref_[...])

    def gate_mix(k, y):
        g = gm_ref[:, k * D_MODEL:(k + 1) * D_MODEL]
        return (0.5 + 0.5 * jnp.tanh(g * 0.5)) * y.astype(BF16)

    ya = gated(ya_ref, za_ref).astype(F32)
    ms = jnp.mean(ya * ya, axis=-1, keepdims=True)
    ya = (ya * lax.rsqrt(ms + NORM_EPS) * ga_ref[...]).astype(BF16)
    mixed = gate_mix(0, _dot(ya, wa_ref[...]))

    yb = gated(ob_ref, zb_ref)
    mixed = mixed + gate_mix(1, _dot(yb, wb_ref[...]))

    oc = oc_ref[...].astype(F32)
    parts = []
    for h in range(GLA_HEADS):
        oh = oc[:, h * GLA_DV:(h + 1) * GLA_DV]
        ms = jnp.mean(oh * oh, axis=-1, keepdims=True)
        parts.append(oh * lax.rsqrt(ms + NORM_EPS))
    yc = (jnp.concatenate(parts, axis=1) * gc_ref[...]).astype(BF16) * silu_b(zc_ref[...])
    mixed = mixed + gate_mix(2, _dot(yc, wc_ref[...]))

    yd = gated(od_ref, zd_ref)
    mixed = mixed + gate_mix(3, _dot(yd, wd_ref[...]))

    x = x_ref[...] + _dot(mixed, wo_ref[...])
    if final_norm:
        ms = jnp.mean(x * x, axis=-1, keepdims=True)
        x = x * lax.rsqrt(ms + NORM_EPS) * fg_ref[...]
    o_ref[...] = x


def _merge(u2, ya, ob, oc, od, x2, wa, wb, wc, wd, wo, ga, gc, fg, layer, final_norm):
    t = x2.shape[0]
    tm = min(512, t)
    row = lambda c: (lambda i: (i, c))
    const2 = lambda i: (0, 0)
    of_layer = lambda i: (layer, 0, 0)
    resident = pl.Buffered(1)
    return pl.pallas_call(
        functools.partial(_merge_kernel, final_norm=final_norm),
        grid=(t // tm,),
        in_specs=[
            pl.BlockSpec((tm, N_BRANCH * D_MODEL), row(COL_GMERGE // (N_BRANCH * D_MODEL))),
            pl.BlockSpec((tm, SSD_INNER), row(COL_ZA // SSD_INNER)),
            pl.BlockSpec((tm, MLA_WIDTH), row(COL_ZB // MLA_WIDTH)),
            pl.BlockSpec((tm, GLA_WIDTH), row(COL_ZC // GLA_WIDTH)),
            pl.BlockSpec((tm, GQA_WIDTH), row(COL_ZD // GQA_WIDTH)),
            pl.BlockSpec((tm, SSD_INNER), row(0)),
            pl.BlockSpec((tm, MLA_WIDTH), row(0)),
            pl.BlockSpec((tm, GLA_WIDTH), row(0)),
            pl.BlockSpec((tm, GQA_WIDTH), row(0)),
            pl.BlockSpec((tm, D_MODEL), row(0)),
            pl.BlockSpec((None, SSD_INNER, D_MODEL), of_layer, pipeline_mode=resident),
            pl.BlockSpec((None, MLA_WIDTH, D_MODEL), of_layer, pipeline_mode=resident),
            pl.BlockSpec((None, GLA_WIDTH, D_MODEL), of_layer, pipeline_mode=resident),
            pl.BlockSpec((None, GQA_WIDTH, D_MODEL), of_layer, pipeline_mode=resident),
            pl.BlockSpec((None, D_MODEL, D_MODEL), of_layer, pipeline_mode=resident),
            pl.BlockSpec((None, 1, SSD_INNER), of_layer),
            pl.BlockSpec((None, 1, GLA_WIDTH), of_layer),
            pl.BlockSpec((1, D_MODEL), const2),
        ],
        out_specs=pl.BlockSpec((tm, D_MODEL), row(0)),
        out_shape=jax.ShapeDtypeStruct((t, D_MODEL), F32),
        compiler_params=_params("parallel"),
        name="merge",
    )(u2, u2, u2, u2, u2, ya, ob, oc, od, x2, wa, wb, wc, wd, wo, ga, gc, fg)


def _reorder_w_in(w):
    w = w.astype(BF16)
    split_idx = [int(v) for v in np.cumsum(IN_WIDTHS)[:-1]]
    (g_merge, z_a, xbc, dt_raw, z_b, q_lat, kv_lat, k_rope, z_c, q_c, k_c, v_c, g_lr,
     z_d, q_d, k_d, v_d) = jnp.split(w, split_idx, axis=-1)
    xs, bm, cm = xbc[..., :SSD_INNER], xbc[..., SSD_INNER:SSD_INNER + SSD_BC], xbc[..., SSD_INNER + SSD_BC:]
    small = jnp.concatenate([dt_raw, g_lr, k_rope, jnp.zeros(w.shape[:-1] + (LANE - 96,), w.dtype)], axis=-1)
    cols = [g_merge, z_a, xs, z_b, z_c, z_d, q_d, v_c, q_lat, small, bm, cm, kv_lat, q_c, k_c, k_d, v_d]
    out = jnp.concatenate(cols, axis=-1)
    assert out.shape[-1] == N_PROJ
    return out


def kernel(x, norm_g, w_in, conv_w, conv_b, a_log, dt_bias, d_skip, ssd_norm_g, q_lat_norm_g, kv_lat_norm_g, w_q_b, w_kv_b, w_gate_up, b_gate, gla_norm_g, q_norm_g, k_norm_g, w_br_a, w_br_b, w_br_c, w_br_d, w_out, final_g):
    B, L, D = x.shape
    depth = w_in.shape[0]
    x2 = x.reshape(B * L, D)
    w_in_b = _reorder_w_in(w_in)
    norm_g3 = norm_g[:, None, :]
    wa, wb, wc, wd, wo = (w.astype(BF16) for w in (w_br_a, w_br_b, w_br_c, w_br_d, w_out))
    ga3, gc3 = ssd_norm_g[:, None, :], gla_norm_g[:, None, :]
    for i in range(depth):
        u2 = _inproj(x2, norm_g3, w_in_b, i)
        u3 = u2.reshape(B, L, N_PROJ)
        ya = _ssd(u3, conv_w[i], conv_b[i], a_log[i], dt_bias[i], d_skip[i])
        ob = _mla(u3, q_lat_norm_g[i], kv_lat_norm_g[i], w_q_b[i], w_kv_b[i])
        oc = _gla(u3, w_gate_up[i], b_gate[i])
        od = _gqa(u3, q_norm_g[i], k_norm_g[i])
        x2 = _merge(
            u2, ya.reshape(B * L, -1), ob.reshape(B * L, -1), oc.reshape(B * L, -1), od.reshape(B * L, -1), x2,
            wa, wb, wc, wd, wo, ga3, gc3, final_g[None, :], layer=i, final_norm=(i == depth - 1))
    return x2.reshape(B, L, D)
```
